```python
import jax, jax.numpy as jnp
from jax import lax
import numpy as np

D_MODEL = 1024
BATCH = 16
SEQ = 2048
DEPTH = 1

M_HEADS = 4
M_QK_DIM = 128
M_V_DIM = 256
M_QK_WIDTH = M_HEADS * M_QK_DIM
M_WIDTH = M_HEADS * M_V_DIM
G_HEADS = 8
G_QK_DIM = 128
G_V_DIM = 128
G_WIDTH = G_HEADS * G_V_DIM
CONV_WIDTH = 4
MIX_WIDTH = M_WIDTH + G_WIDTH
CHUNK = 64
EPS = 1e-6
IN_SPLITS = (M_QK_WIDTH, M_QK_WIDTH, M_WIDTH, M_WIDTH, M_WIDTH, M_HEADS, M_HEADS,
             3 * G_WIDTH, G_WIDTH, G_HEADS, G_HEADS)
IN_COLS = sum(IN_SPLITS)

kernel_name = "hymba_style_mlstm_gdn_hybrid"


def rms_norm(x, g):
    xf = x.astype(jnp.float32)
    y = xf * lax.rsqrt(jnp.mean(xf * xf, axis=-1, keepdims=True) + EPS)
    return (y * g.astype(jnp.float32)).astype(x.dtype)


def l2_norm(x):
    return x * lax.rsqrt(jnp.sum(x * x, axis=-1, keepdims=True) + EPS)


def split_heads(a, n_heads):
    b, s, w = a.shape
    return a.reshape(b, s, n_heads, w // n_heads).transpose(0, 2, 1, 3)


def merge_heads(a):
    b, h, s, d = a.shape
    return a.transpose(0, 2, 1, 3).reshape(b, s, h * d)


def to_chunks(a):
    b, h, s = a.shape[:3]
    n = s // CHUNK
    return jnp.moveaxis(a.reshape(b, h, n, CHUNK, *a.shape[3:]), 2, 0)


def from_chunks(a):
    n, b, h, l, d = a.shape
    return jnp.moveaxis(a, 0, 2).reshape(b, h, n * l, d)


def causal_depthwise_conv(x, w):
    k, c = w.shape
    return lax.conv_general_dilated(
        x, w[:, None, :].astype(x.dtype), window_strides=(1,), padding=[(k - 1, 0)],
        dimension_numbers=('NWC', 'WIO', 'NWC'), feature_group_count=c)


def mlstm_chunkwise(q, k, v, i_pre, f_pre):
    b, h, s, dk = q.shape
    dv = v.shape[-1]
    q = q * (dk ** -0.5)
    log_f = jax.nn.log_sigmoid(f_pre)
    causal = jnp.tril(jnp.ones((CHUNK, CHUNK), dtype=bool))

    def step(carry, inp):
        C, n, m = carry
        qi, ki, vi, ii, lf = inp
        bcum = jnp.cumsum(lf, axis=-1)
        D = jnp.where(causal, bcum[..., :, None] - bcum[..., None, :] + ii[..., None, :], -jnp.inf)
        inter = bcum + m[..., None]
        m_t = jnp.maximum(inter, jnp.max(D, axis=-1))
        w_inter = jnp.exp(inter - m_t)
        P = jnp.exp(D - m_t[..., None]) * jnp.einsum('bhld,bhsd->bhls', qi, ki)
        num = (w_inter[..., None] * jnp.einsum('bhld,bhde->bhle', qi, C)
               + jnp.einsum('bhls,bhse->bhle', P, vi))
        den = w_inter * jnp.einsum('bhld,bhd->bhl', qi, n) + jnp.sum(P, axis=-1)
        h_out = num / jnp.maximum(jnp.abs(den), jnp.exp(-m_t))[..., None]
        b_last = bcum[..., -1]
        a_s = b_last[..., None] - bcum + ii
        m_new = jnp.maximum(b_last + m, jnp.max(a_s, axis=-1))
        decay = jnp.exp(b_last + m - m_new)
        ws = jnp.exp(a_s - m_new[..., None])
        C = decay[..., None, None] * C + jnp.einsum('bhs,bhsd,bhse->bhde', ws, ki, vi)
        n = decay[..., None] * n + jnp.einsum('bhs,bhsd->bhd', ws, ki)
        return (C, n, m_new), h_out

    init = (jnp.zeros((b, h, dk, dv), jnp.float32), jnp.zeros((b, h, dk), jnp.float32),
            jnp.zeros((b, h), jnp.float32))
    xs = (to_chunks(q), to_chunks(k), to_chunks(v), to_chunks(i_pre), to_chunks(log_f))
    _, hs = lax.scan(step, init, xs)
    return from_chunks(hs)


def gated_delta_chunkwise(q, k, v, log_alpha, beta):
    b, h, s, dk = q.shape
    dv = v.shape[-1]
    q = q * (dk ** -0.5)
    qc, kc, vc = to_chunks(q), to_chunks(k), to_chunks(v)
    lac, bc = to_chunks(log_alpha), to_chunks(beta)
    g = jnp.cumsum(lac, axis=-1)
    incl = jnp.tril(jnp.ones((CHUNK, CHUNK), dtype=bool))
    strict = jnp.tril(jnp.ones((CHUNK, CHUNK), dtype=bool), k=-1)
    gamma = jnp.exp(jnp.where(incl, g[..., :, None] - g[..., None, :], -jnp.inf))
    kb = kc * bc[..., None]
    vb = vc * bc[..., None]
    lower = jnp.where(strict, jnp.einsum('nbhid,nbhjd->nbhij', kb, kc) * gamma, 0.0)
    a_mat = lower + jnp.eye(CHUNK, dtype=lower.dtype)
    rhs = jnp.concatenate([vb, kb * jnp.exp(g)[..., None]], axis=-1)
    sol = lax.linalg.triangular_solve(a_mat, rhs, left_side=True, lower=True, unit_diagonal=True)
    U, W = sol[..., :dv], sol[..., dv:]
    A_qk = jnp.where(incl, jnp.einsum('nbhid,nbhjd->nbhij', qc, kc) * gamma, 0.0)

    def step(S, inp):
        qi, ki, Ui, Wi, Ai, gi = inp
        v_new = Ui - jnp.einsum('bhld,bhde->bhle', Wi, S)
        o = (jnp.einsum('bhld,bhde->bhle', qi * jnp.exp(gi)[..., None], S)
             + jnp.einsum('bhls,bhse->bhle', Ai, v_new))
        g_last = gi[..., -1]
        S = (jnp.exp(g_last)[..., None, None] * S
             + jnp.einsum('bhld,bhle->bhde', ki * jnp.exp(g_last[..., None] - gi)[..., None], v_new))
        return S, o

    S0 = jnp.zeros((b, h, dk, dv), jnp.float32)
    _, os_ = lax.scan(step, S0, (qc, kc, U, W, A_qk, g))
    return from_chunks(os_)


def setup_inputs(seed: int = 0) -> dict:
    key = jax.random.key(seed)
    ks = jax.random.split(key, 13)
    f32 = jnp.float32
    x = jax.random.normal(ks[0], (BATCH, SEQ, D_MODEL), f32)
    attn_norm = 1.0 + 0.02 * jax.random.normal(ks[1], (DEPTH, D_MODEL), f32)
    w_in = jax.random.normal(ks[2], (DEPTH, D_MODEL, IN_COLS), f32) * (D_MODEL ** -0.5)
    m_i_bias = 0.1 * jax.random.normal(ks[3], (DEPTH, M_HEADS), f32)
    m_f_bias = (jnp.linspace(3.0, 6.0, M_HEADS, dtype=f32)[None, :]
                + 0.1 * jax.random.normal(ks[4], (DEPTH, M_HEADS), f32))
    m_out_norm = 1.0 + 0.02 * jax.random.normal(ks[5], (DEPTH, M_WIDTH), f32)
    g_conv = jax.random.normal(ks[6], (DEPTH, CONV_WIDTH, 3 * G_WIDTH), f32) * (CONV_WIDTH ** -0.5)
    g_a_log = jnp.log(jax.random.uniform(ks[7], (DEPTH, G_HEADS), f32, 1.0, 16.0))
    dt = jnp.exp(jax.random.uniform(ks[8], (DEPTH, G_HEADS), f32, np.log(1e-3), np.log(1e-1)))
    g_dt_bias = dt + jnp.log(-jnp.expm1(-dt))
    g_out_norm = 1.0 + 0.02 * jax.random.normal(ks[9], (DEPTH, G_V_DIM), f32)
    w_out = jax.random.normal(ks[10], (DEPTH, MIX_WIDTH, D_MODEL), f32) * (MIX_WIDTH ** -0.5)
    final_norm = 1.0 + 0.02 * jax.random.normal(ks[11], (D_MODEL,), f32)
    return {"x": x, "attn_norm": attn_norm, "w_in": w_in, "m_i_bias": m_i_bias,
            "m_f_bias": m_f_bias, "m_out_norm": m_out_norm, "g_conv": g_conv,
            "g_a_log": g_a_log, "g_dt_bias": g_dt_bias, "g_out_norm": g_out_norm,
            "w_out": w_out, "final_norm": final_norm}


def reference(x, attn_norm, w_in, m_i_bias, m_f_bias, m_out_norm, g_conv, g_a_log,
              g_dt_bias, g_out_norm, w_out, final_norm):
    f32 = jnp.float32
    split_points = [int(p) for p in np.cumsum(IN_SPLITS)[:-1]]
    for l in range(DEPTH):
        h = rms_norm(x, attn_norm[l])
        proj = jnp.einsum('bsd,dp->bsp', h, w_in[l])
        (mq, mk, mv, mo, mz, mi, mf, gqkv, gz, gb, ga) = jnp.split(proj, split_points, axis=-1)

        q_m = split_heads(mq, M_HEADS).astype(f32)
        k_m = split_heads(mk, M_HEADS).astype(f32)
        v_m = split_heads(mv, M_HEADS).astype(f32)
        i_pre = (mi.astype(f32) + m_i_bias[l].astype(f32)).transpose(0, 2, 1)
        f_pre = (mf.astype(f32) + m_f_bias[l].astype(f32)).transpose(0, 2, 1)
        h_m = mlstm_chunkwise(q_m, k_m, v_m, i_pre, f_pre)
        h_m = h_m * lax.rsqrt(jnp.mean(h_m * h_m, axis=-1, keepdims=True) + EPS)
        h_m = merge_heads(h_m) * m_out_norm[l].astype(f32)
        y_m = h_m * jax.nn.sigmoid(mo.astype(f32)) * jax.nn.silu(mz.astype(f32))

        qkv = jax.nn.silu(causal_depthwise_conv(gqkv, g_conv[l]).astype(f32))
        gq, gk, gv = jnp.split(qkv, [G_HEADS * G_QK_DIM, 2 * G_HEADS * G_QK_DIM], axis=-1)
        q_g = l2_norm(split_heads(gq, G_HEADS))
        k_g = l2_norm(split_heads(gk, G_HEADS))
        v_g = split_heads(gv, G_HEADS)
        beta = jax.nn.sigmoid(gb.astype(f32)).transpose(0, 2, 1)
        log_alpha = (-jnp.exp(g_a_log[l].astype(f32))
                     * jax.nn.softplus(ga.astype(f32) + g_dt_bias[l].astype(f32))).transpose(0, 2, 1)
        o_g = gated_delta_chunkwise(q_g, k_g, v_g, log_alpha, beta)
        o_g = (o_g * lax.rsqrt(jnp.mean(o_g * o_g, axis=-1, keepdims=True) + EPS)
               * g_out_norm[l].astype(f32))
        y_g = merge_heads(o_g) * jax.nn.silu(gz.astype(f32))

        mix = jnp.concatenate([y_m, y_g], axis=-1).astype(x.dtype)
        x = x + jnp.einsum('bsm,md->bsd', mix, w_out[l])
    return rms_norm(x, final_norm)
```

```python
import functools

import jax
import jax.numpy as jnp
from jax import lax
from jax.experimental import pallas as pl
from jax.experimental.pallas import tpu as pltpu

F32 = jnp.float32
BF16 = jnp.bfloat16

D_MODEL = 1024
M_HEADS = 4
M_QK_DIM = 128
M_V_DIM = 256
M_QK_WIDTH = M_HEADS * M_QK_DIM
M_WIDTH = M_HEADS * M_V_DIM
G_HEADS = 8
G_DIM = 128
G_WIDTH = G_HEADS * G_DIM
CONV_WIDTH = 4
MIX_WIDTH = M_WIDTH + G_WIDTH
CHUNK = 64
EPS = 1e-6
LANES = 128

OFF_MQ = 0
OFF_MK = OFF_MQ + M_QK_WIDTH
OFF_MV = OFF_MK + M_QK_WIDTH
OFF_MO = OFF_MV + M_WIDTH
OFF_MZ = OFF_MO + M_WIDTH
OFF_GQKV = OFF_MZ + M_WIDTH
OFF_GZ = OFF_GQKV + 3 * G_WIDTH
N_MAIN = OFF_GZ + G_WIDTH
N_GATE = 2 * LANES
M_AUG = M_V_DIM + LANES

PROJ_ROWS = 512
PROJ_COLS = 1024
MIX_ROWS = 256
VMEM_LIMIT = 56 * 1024 * 1024


def _dot(a, b):
    return jnp.dot(a, b, preferred_element_type=F32)


def _dot_nt(a, b):
    return lax.dot_general(a, b, (((1,), (1,)), ((), ())), preferred_element_type=F32)


def _dot_tn(a, b):
    return lax.dot_general(a, b, (((0,), (0,)), ((), ())), preferred_element_type=F32)


def _proj_kernel(x_ref, g_ref, wm_ref, wg_ref, pm_ref, gt_ref, h_scr):
    x = x_ref[...]
    ms = jnp.mean(x * x, axis=-1, keepdims=True)
    h_scr[...] = (x * lax.rsqrt(ms + EPS) * g_ref[...]).astype(BF16)
    for nb in range(N_MAIN // PROJ_COLS):
        cols = slice(nb * PROJ_COLS, (nb + 1) * PROJ_COLS)
        pm_ref[:, cols] = _dot(h_scr[...], wm_ref[:, cols]).astype(BF16)
    gt_ref[...] = _dot(h_scr[...], wg_ref[...])


def _input_projection(x2, gain, w_main, w_gate):
    rows = x2.shape[0]
    tm = min(PROJ_ROWS, rows)
    const = lambda i: (0, 0)
    return pl.pallas_call(
        _proj_kernel,
        grid=(rows // tm,),
        in_specs=[
            pl.BlockSpec((tm, D_MODEL), lambda i: (i, 0)),
            pl.BlockSpec((1, D_MODEL), const),
            pl.BlockSpec((D_MODEL, N_MAIN), const, pipeline_mode=pl.Buffered(1)),
            pl.BlockSpec((D_MODEL, N_GATE), const, pipeline_mode=pl.Buffered(1)),
        ],
        out_specs=[
            pl.BlockSpec((tm, N_MAIN), lambda i: (i, 0)),
            pl.BlockSpec((tm, N_GATE), lambda i: (i, 0)),
        ],
        out_shape=[
            jax.ShapeDtypeStruct((rows, N_MAIN), BF16),
            jax.ShapeDtypeStruct((rows, N_GATE), F32),
        ],
        scratch_shapes=[pltpu.VMEM((tm, D_MODEL), BF16)],
        compiler_params=pltpu.CompilerParams(
            dimension_semantics=("arbitrary",), vmem_limit_bytes=VMEM_LIMIT),
    )(x2, gain, w_main, w_gate)


def _scan64(x, row, combine, identity):
    shift = 1
    while shift < CHUNK:
        x = combine(x, jnp.where(row >= shift, pltpu.roll(x, shift, 0), identity))
        shift *= 2
    return x


def _softplus(x):
    return jnp.maximum(x, 0.0) + jnp.log1p(jnp.exp(-jnp.abs(x)))


def _lane_col(x, j, width):
    return jnp.broadcast_to(x[:, j:j + 1], (x.shape[0], width))


def _mixer_kernel(pm_ref, gt_ref, pv_ref, conv_ref, mnorm_ref, gnorm_ref, mix_ref,
                  caug_scr, m_scr, s_scr, xbuf_scr, gq_scr, gk_scr, gv_scr,
                  lg_scr, ab_scr, hm_scr, og_scr, *, rows):
    n_chunks = rows // CHUNK
    t = pl.program_id(1)

    @pl.when(t == 0)
    def _():
        caug_scr[...] = jnp.zeros_like(caug_scr)
        m_scr[...] = jnp.zeros_like(m_scr)
        s_scr[...] = jnp.zeros_like(s_scr)
        xbuf_scr[0:8, :] = jnp.zeros((8, 3 * G_WIDTH), F32)

    lane = lax.broadcasted_iota(jnp.int32, (rows, LANES), 1)
    is_m = lane < M_HEADS
    a = gt_ref[:, 0:LANES] + pv_ref[0:1, :]
    b = gt_ref[:, LANES:2 * LANES] + pv_ref[1:2, :]
    log_f = -_softplus(-b)
    log_alpha = -jnp.exp(pv_ref[2:3, :]) * _softplus(b)
    lg_scr[...] = jnp.where(is_m, log_f, log_alpha)
    ab_scr[...] = jnp.where(is_m, a, jax.nn.sigmoid(a))

    xbuf_scr[8:8 + rows, :] = pm_ref[:, OFF_GQKV:OFF_GQKV + 3 * G_WIDTH].astype(F32)
    ones_bd = jnp.where(
        lax.broadcasted_iota(jnp.int32, (2 * LANES, 2 * LANES), 0) // LANES
        == lax.broadcasted_iota(jnp.int32, (2 * LANES, 2 * LANES), 1) // LANES, 1.0, 0.0).astype(BF16)
    for part, dst in enumerate((gq_scr, gk_scr, gv_scr)):
        for cb in range(G_WIDTH // (2 * LANES)):
            cols = slice(part * G_WIDTH + cb * 2 * LANES, part * G_WIDTH + (cb + 1) * 2 * LANES)
            acc = conv_ref[CONV_WIDTH - 1:CONV_WIDTH, cols] * xbuf_scr[8:8 + rows, cols]
            for j in range(CONV_WIDTH - 1):
                off = 8 - (CONV_WIDTH - 1) + j
                acc = acc + conv_ref[j:j + 1, cols] * xbuf_scr[off:off + rows, cols]
            act = acc * jax.nn.sigmoid(acc)
            if part < 2:
                ssq = _dot((act * act).astype(BF16), ones_bd)
                act = act * lax.rsqrt(ssq + EPS)
            dst[:, cb * 2 * LANES:(cb + 1) * 2 * LANES] = act
    xbuf_scr[0:8, :] = xbuf_scr[rows:rows + 8, :]

    row = lax.broadcasted_iota(jnp.int32, (CHUNK, LANES), 0)
    ri = lax.broadcasted_iota(jnp.int32, (CHUNK, CHUNK), 0)
    ci = lax.broadcasted_iota(jnp.int32, (CHUNK, CHUNK), 1)
    incl = ri >= ci
    strict = ri > ci
    eye = jnp.where(ri == ci, 1.0, 0.0).astype(F32)
    ones_v = jnp.ones((CHUNK, LANES), BF16)
    lane_c = lax.broadcasted_iota(jnp.int32, (CHUNK, LANES), 1)
    is_m_c = lane_c < M_HEADS
    m_scale = M_QK_DIM ** -0.5
    g_scale = G_DIM ** -0.5

    def chunk_body(c, carry):
        r0 = pl.multiple_of(c * CHUNK, CHUNK)
        rs = pl.ds(r0, CHUNK)
        lg = lg_scr[rs, :]
        ab = ab_scr[rs, :]
        cum = _scan64(lg, row, jnp.add, 0.0)
        m_prev = m_scr[...]
        cmax = _scan64(ab - cum, row, jnp.maximum, -jnp.inf)
        inter = cum + m_prev
        m_t = jnp.maximum(inter, cum + cmax)
        w_inter = jnp.exp(inter - m_t) * m_scale
        e_neg_mt = jnp.exp(-m_t)
        last = cum[CHUNK - 1:CHUNK, :]
        a_s = last - cum + ab
        m_new = jnp.maximum(last + m_prev, jnp.max(a_s, axis=0, keepdims=True))
        decay_m = jnp.exp(last + m_prev - m_new)
        w_s = jnp.exp(a_s - m_new)
        m_scr[...] = jnp.where(is_m_c[0:1, :], m_new, 0.0)
        e_g = jnp.exp(cum)
        e_last = jnp.exp(last)
        e_rest = jnp.exp(last - cum)
        row_form = jnp.where(is_m_c, ab - cum, -cum).T

        for h in range(M_HEADS):
            q = pm_ref[rs, OFF_MQ + h * M_QK_DIM:OFF_MQ + (h + 1) * M_QK_DIM]
            k = pm_ref[rs, OFF_MK + h * M_QK_DIM:OFF_MK + (h + 1) * M_QK_DIM]
            v = pm_ref[rs, OFF_MV + h * M_V_DIM:OFF_MV + (h + 1) * M_V_DIM]
            v_aug = jnp.concatenate([v, ones_v], axis=1)
            s_qk = _dot_nt(q, k) * m_scale
            d_mat = _lane_col(cum, h, CHUNK) + row_form[h:h + 1, :]
            p = jnp.where(incl, jnp.exp(d_mat - _lane_col(m_t, h, CHUNK)), 0.0) * s_qk
            c_aug = caug_scr[h]
            nd = (jnp.concatenate([_lane_col(w_inter, h, LANES)] * 3, axis=1) * _dot(q, c_aug.astype(BF16))
                  + _dot(p.astype(BF16), v_aug))
            den = jnp.maximum(jnp.abs(nd[:, M_V_DIM:]), _lane_col(e_neg_mt, h, LANES))
            hm_scr[rs, h * M_V_DIM:(h + 1) * M_V_DIM] = nd[:, :M_V_DIM] / jnp.concatenate([den, den], axis=1)
            kw = (k.astype(F32) * _lane_col(w_s, h, LANES)).astype(BF16)
            dec = jnp.broadcast_to(decay_m[0:1, h:h + 1], (M_QK_DIM, M_AUG))
            caug_scr[h] = dec * c_aug + _dot_tn(kw, v_aug)

        for j in range(G_HEADS):
            ln = M_HEADS + j
            cols = slice(j * G_DIM, (j + 1) * G_DIM)
            q = gq_scr[rs, cols]
            k = gk_scr[rs, cols]
            v = gv_scr[rs, cols]
            k_b = k.astype(BF16)
            kk = _dot_nt(k_b, k_b)
            qk = _dot_nt(q.astype(BF16), k_b)
            gam = jnp.where(incl, jnp.exp(_lane_col(cum, ln, CHUNK) + row_form[ln:ln + 1, :]), 0.0)
            beta64 = _lane_col(ab, ln, CHUNK)
            x = jnp.where(strict, -(beta64 * kk * gam), 0.0)
            a_qk = qk * gam * g_scale
            t_inv = eye + x
            pw = x
            for _ in range(5):
                pw_b = pw.astype(BF16)
                pw = _dot(pw_b, pw_b)
                t_inv = t_inv + _dot(t_inv.astype(BF16), pw.astype(BF16))
            beta = _lane_col(ab, ln, LANES)
            rhs = jnp.concatenate([v * beta, k * (beta * _lane_col(e_g, ln, LANES))], axis=1)
            sol = _dot(t_inv.astype(BF16), rhs.astype(BF16))
            s_old = s_scr[j]
            s_b = s_old.astype(BF16)
            v_new = sol[:, :G_DIM] - _dot(sol[:, G_DIM:].astype(BF16), s_b)
            v_new_b = v_new.astype(BF16)
            qg = (q * (_lane_col(e_g, ln, LANES) * g_scale)).astype(BF16)
            og_scr[rs, cols] = _dot(qg, s_b) + _dot(a_qk.astype(BF16), v_new_b)
            kd = (k * _lane_col(e_rest, ln, LANES)).astype(BF16)
            dec = jnp.broadcast_to(e_last[0:1, ln:ln + 1], (G_DIM, G_DIM))
            s_scr[j] = dec * s_old + _dot_tn(kd, v_new_b)
        return carry

    lax.fori_loop(0, n_chunks, chunk_body, 0)

    ones_full = jnp.ones((M_V_DIM, M_V_DIM), BF16)
    for h in range(M_HEADS):
        cols = slice(h * M_V_DIM, (h + 1) * M_V_DIM)
        hm = hm_scr[:, cols]
        ms = _dot((hm * hm).astype(BF16), ones_full) * (1.0 / M_V_DIM)
        o_gate = jax.nn.sigmoid(pm_ref[:, OFF_MO + h * M_V_DIM:OFF_MO + (h + 1) * M_V_DIM].astype(F32))
        z = pm_ref[:, OFF_MZ + h * M_V_DIM:OFF_MZ + (h + 1) * M_V_DIM].astype(F32)
        y = hm * lax.rsqrt(ms + EPS) * mnorm_ref[0:1, cols] * o_gate * (z * jax.nn.sigmoid(z))
        mix_ref[:, cols] = y.astype(BF16)
    for cb in range(G_WIDTH // (2 * LANES)):
        cols = slice(cb * 2 * LANES, (cb + 1) * 2 * LANES)
        og = og_scr[:, cols]
        ms = _dot((og * og).astype(BF16), ones_bd) * (1.0 / G_DIM)
        z = pm_ref[:, OFF_GZ + cb * 2 * LANES:OFF_GZ + (cb + 1) * 2 * LANES].astype(F32)
        y = og * lax.rsqrt(ms + EPS) * gnorm_ref[0:1, cols] * (z * jax.nn.sigmoid(z))
        mix_ref[:, M_WIDTH + cb * 2 * LANES:M_WIDTH + (cb + 1) * 2 * LANES] = y.astype(BF16)


def _mixers(pm, gt, pvec, conv_w, mnorm, gnorm, batch, seq):
    rows = min(MIX_ROWS, seq)
    steps = seq // rows
    const = lambda b, t: (0, 0)
    tok = lambda b, t: (b * steps + t, 0)
    return pl.pallas_call(
        functools.partial(_mixer_kernel, rows=rows),
        grid=(batch, steps),
        in_specs=[
            pl.BlockSpec((rows, N_MAIN), tok),
            pl.BlockSpec((rows, N_GATE), tok),
            pl.BlockSpec((8, LANES), const),
            pl.BlockSpec((CONV_WIDTH, 3 * G_WIDTH), const),
            pl.BlockSpec((1, M_WIDTH), const),
            pl.BlockSpec((1, G_WIDTH), const),
        ],
        out_specs=pl.BlockSpec((rows, MIX_WIDTH), tok),
        out_shape=jax.ShapeDtypeStruct((batch * seq, MIX_WIDTH), BF16),
        scratch_shapes=[
            pltpu.VMEM((M_HEADS, M_QK_DIM, M_AUG), F32),
            pltpu.VMEM((1, LANES), F32),
            pltpu.VMEM((G_HEADS, G_DIM, G_DIM), F32),
            pltpu.VMEM((rows + 8, 3 * G_WIDTH), F32),
            pltpu.VMEM((rows, G_WIDTH), F32),
            pltpu.VMEM((rows, G_WIDTH), F32),
            pltpu.VMEM((rows, G_WIDTH), F32),
            pltpu.VMEM((rows, LANES), F32),
            pltpu.VMEM((rows, LANES), F32),
            pltpu.VMEM((rows, M_WIDTH), F32),
            pltpu.VMEM((rows, G_WIDTH), F32),
        ],
        compiler_params=pltpu.CompilerParams(
            dimension_semantics=("arbitrary", "arbitrary"), vmem_limit_bytes=VMEM_LIMIT),
    )(pm, gt, pvec, conv_w, mnorm, gnorm)


def _out_kernel(mix_ref, x_ref, wo_ref, fn_ref, o_ref):
    y = x_ref[...] + _dot(mix_ref[...], wo_ref[...])
    ms = jnp.mean(y * y, axis=-1, keepdims=True)
    o_ref[...] = y * lax.rsqrt(ms + EPS) * fn_ref[...]


def _output_projection(mix, x2, w_out, final_gain):
    rows = x2.shape[0]
    tm = min(PROJ_ROWS, rows)
    const = lambda i: (0, 0)
    return pl.pallas_call(
        _out_kernel,
        grid=(rows // tm,),
        in_specs=[
            pl.BlockSpec((tm, MIX_WIDTH), lambda i: (i, 0)),
            pl.BlockSpec((tm, D_MODEL), lambda i: (i, 0)),
            pl.BlockSpec((MIX_WIDTH, D_MODEL), const),
            pl.BlockSpec((1, D_MODEL), const),
        ],
        out_specs=pl.BlockSpec((tm, D_MODEL), lambda i: (i, 0)),
        out_shape=jax.ShapeDtypeStruct((rows, D_MODEL), F32),
        compiler_params=pltpu.CompilerParams(
            dimension_semantics=("arbitrary",), vmem_limit_bytes=VMEM_LIMIT),
    )(mix, x2, w_out, final_gain)


def _pad_lanes(v):
    return jnp.pad(v.astype(F32), (0, LANES - v.shape[0]))


def kernel(x, attn_norm, w_in, m_i_bias, m_f_bias, m_out_norm, g_conv, g_a_log, g_dt_bias,
           g_out_norm, w_out, final_norm):
    batch, seq, _ = x.shape
    assert attn_norm.shape[0] == 1, "single-layer block"
    assert seq % CHUNK == 0
    w = w_in[0]
    splits = (M_QK_WIDTH, M_QK_WIDTH, M_WIDTH, M_WIDTH, M_WIDTH, M_HEADS, M_HEADS,
              3 * G_WIDTH, G_WIDTH, G_HEADS, G_HEADS)
    offs = [0]
    for s in splits:
        offs.append(offs[-1] + s)
    mq, mk, mv, mo, mz, mi, mf, gqkv, gz, gb, ga = (w[:, offs[i]:offs[i + 1]] for i in range(len(splits)))
    w_main = jnp.concatenate([mq, mk, mv, mo, mz, gqkv, gz], axis=1).astype(BF16)
    zpad = jnp.zeros((D_MODEL, LANES - M_HEADS - G_HEADS), w.dtype)
    w_gate = jnp.concatenate([mi, gb, zpad, mf, ga, zpad], axis=1).astype(BF16)
    zero4 = jnp.zeros((M_HEADS,), F32)
    pvec = jnp.stack([
        _pad_lanes(m_i_bias[0]),
        _pad_lanes(jnp.concatenate([m_f_bias[0].astype(F32), g_dt_bias[0].astype(F32)])),
        _pad_lanes(jnp.concatenate([zero4, g_a_log[0].astype(F32)])),
    ] + [jnp.zeros((LANES,), F32)] * 5)
    gnorm = jnp.tile(g_out_norm[0].astype(F32), G_HEADS)[None, :]

    x2 = x.reshape(batch * seq, D_MODEL)
    pm, gt = _input_projection(x2, attn_norm.astype(F32), w_main, w_gate)
    mix = _mixers(pm, gt, pvec, g_conv[0].astype(F32), m_out_norm.astype(F32), gnorm, batch, seq)
    out = _output_projection(mix, x2, w_out[0].astype(BF16), final_norm.astype(F32)[None, :])
    return out.reshape(batch, seq, D_MODEL)
```

```python
import functools

import jax
import jax.numpy as jnp
import numpy as np
from jax import lax
from jax.experimental import pallas as pl
from jax.experimental.pallas import tpu as pltpu

F32 = jnp.float32
BF16 = jnp.bfloat16

D_MODEL = 1024
M_HEADS = 4
M_QK_DIM = 128
M_V_DIM = 256
M_QK_WIDTH = M_HEADS * M_QK_DIM
M_WIDTH = M_HEADS * M_V_DIM
G_HEADS = 8
G_DIM = 128
G_WIDTH = G_HEADS * G_DIM
CONV_WIDTH = 4
MIX_WIDTH = M_WIDTH + G_WIDTH
EPS = 1e-6
LANES = 128

OFF_MQ = 0
OFF_MK = OFF_MQ + M_QK_WIDTH
OFF_MV = OFF_MK + M_QK_WIDTH
OFF_MO = OFF_MV + M_WIDTH
OFF_MZ = OFF_MO + M_WIDTH
OFF_GQKV = OFF_MZ + M_WIDTH
OFF_GZ = OFF_GQKV + 3 * G_WIDTH
N_MAIN = OFF_GZ + G_WIDTH
N_GATE = 2 * LANES
M_AUG = M_V_DIM + LANES

PROJ_ROWS = 512
PROJ_COLS = 1024
TILE = 256
N_LEVELS = 8
VMEM_LIMIT = 56 * 1024 * 1024

M_BC = 4
G_BC = 4


def _dot(a, b):
    return jnp.dot(a, b, preferred_element_type=F32)


def _dot_nt(a, b):
    return lax.dot_general(a, b, (((1,), (1,)), ((), ())), preferred_element_type=F32)


def _dot_tn(a, b):
    return lax.dot_general(a, b, (((0,), (0,)), ((), ())), preferred_element_type=F32)


def _proj_kernel(x_ref, g_ref, wm_ref, wg_ref, pm_ref, gt_ref, h_scr):
    x = x_ref[...]
    ms = jnp.mean(x * x, axis=-1, keepdims=True)
    h_scr[...] = (x * lax.rsqrt(ms + EPS) * g_ref[...]).astype(BF16)
    for nb in range(N_MAIN // PROJ_COLS):
        cols = slice(nb * PROJ_COLS, (nb + 1) * PROJ_COLS)
        pm_ref[:, cols] = _dot(h_scr[...], wm_ref[:, cols]).astype(BF16)
    gt_ref[...] = _dot(h_scr[...], wg_ref[...])


def _input_projection(x2, gain, w_main, w_gate):
    rows = x2.shape[0]
    tm = min(PROJ_ROWS, rows)
    const = lambda i: (0, 0)
    return pl.pallas_call(
        _proj_kernel,
        grid=(rows // tm,),
        in_specs=[
            pl.BlockSpec((tm, D_MODEL), lambda i: (i, 0)),
            pl.BlockSpec((1, D_MODEL), const),
            pl.BlockSpec((D_MODEL, N_MAIN), const, pipeline_mode=pl.Buffered(1)),
            pl.BlockSpec((D_MODEL, N_GATE), const, pipeline_mode=pl.Buffered(1)),
        ],
        out_specs=[
            pl.BlockSpec((tm, N_MAIN), lambda i: (i, 0)),
            pl.BlockSpec((tm, N_GATE), lambda i: (i, 0)),
        ],
        out_shape=[
            jax.ShapeDtypeStruct((rows, N_MAIN), BF16),
            jax.ShapeDtypeStruct((rows, N_GATE), F32),
        ],
        scratch_shapes=[pltpu.VMEM((tm, D_MODEL), BF16)],
        compiler_params=pltpu.CompilerParams(
            dimension_semantics=("arbitrary",), vmem_limit_bytes=VMEM_LIMIT),
    )(x2, gain, w_main, w_gate)


def _scan_rows(x, row, combine, identity):
    shift = 1
    while shift < TILE:
        x = combine(x, jnp.where(row >= shift, pltpu.roll(x, shift, 0), identity))
        shift *= 2
    return x


def _softplus(x):
    return jnp.maximum(x, 0.0) + jnp.log1p(jnp.exp(-jnp.abs(x)))


def _level_masks():
    i = np.arange(TILE)[:, None]
    j = np.arange(TILE)[None, :]
    out = []
    for l in range(N_LEVELS):
        s = 1 << l
        out.append((i // (2 * s) == j // (2 * s)) & ((i // s) % 2 == 1) & ((j // s) % 2 == 0))
    out.append(i == j)
    return jnp.asarray(np.stack(out).astype(np.float32))


def _mixer_kernel(pm_ref, gt_ref, pv_ref, conv_ref, mnorm_ref, gnorm_ref, mask_ref, mix_ref,
                  caug_scr, m_scr, s_scr, xbuf_scr, gq_scr, gk_scr, gv_scr, bc_scr):
    t = pl.program_id(1)

    @pl.when(t == 0)
    def _():
        caug_scr[...] = jnp.zeros_like(caug_scr)
        m_scr[...] = jnp.zeros_like(m_scr)
        s_scr[...] = jnp.zeros_like(s_scr)
        xbuf_scr[0:8, :] = jnp.zeros((8, 3 * G_WIDTH), F32)

    lane = lax.broadcasted_iota(jnp.int32, (TILE, LANES), 1)
    row = lax.broadcasted_iota(jnp.int32, (TILE, LANES), 0)
    is_m = lane < M_HEADS
    a = gt_ref[:, 0:LANES] + pv_ref[0:1, :]
    b = gt_ref[:, LANES:2 * LANES] + pv_ref[1:2, :]
    log_f = -_softplus(-b)
    log_alpha = -jnp.exp(pv_ref[2:3, :]) * _softplus(b)
    ab = jnp.where(is_m, a, jax.nn.sigmoid(a))
    cum = _scan_rows(jnp.where(is_m, log_f, log_alpha), row, jnp.add, 0.0)
    m_prev = m_scr[...]
    cmax = _scan_rows(ab - cum, row, jnp.maximum, -jnp.inf)
    inter = cum + m_prev
    m_t = jnp.maximum(inter, cum + cmax)
    last = cum[TILE - 1:TILE, :]
    a_s = last - cum + ab
    m_new = jnp.maximum(last + m_prev, jnp.max(a_s, axis=0, keepdims=True))
    decay_m = jnp.exp(last + m_prev - m_new)
    e_last = jnp.exp(last)
    m_scr[...] = jnp.where(is_m[0:1, :], m_new, 0.0)
    row_form = jnp.where(is_m, ab - cum + np.float32(np.log(M_QK_DIM ** -0.5)), -cum).T

    m_cols = (cum - m_t, jnp.exp(inter - m_t) * (M_QK_DIM ** -0.5), jnp.exp(-m_t), jnp.exp(a_s - m_new))
    g_cols = (cum, ab, jnp.exp(cum), jnp.exp(last - cum))
    for h in range(M_HEADS):
        for i, colv in enumerate(m_cols):
            bc_scr[h * M_BC + i] = jnp.broadcast_to(colv[:, h:h + 1], (TILE, LANES))
    for j in range(G_HEADS):
        for i, colv in enumerate(g_cols):
            bc_scr[M_HEADS * M_BC + j * G_BC + i] = jnp.broadcast_to(
                colv[:, M_HEADS + j:M_HEADS + j + 1], (TILE, LANES))

    xbuf_scr[8:8 + TILE, :] = pm_ref[:, OFF_GQKV:OFF_GQKV + 3 * G_WIDTH].astype(F32)
    ones_bd = jnp.where(
        lax.broadcasted_iota(jnp.int32, (2 * LANES, 2 * LANES), 0) // LANES
        == lax.broadcasted_iota(jnp.int32, (2 * LANES, 2 * LANES), 1) // LANES, 1.0, 0.0).astype(BF16)
    for part, dst in enumerate((gq_scr, gk_scr, gv_scr)):
        for cb in range(G_WIDTH // (2 * LANES)):
            cols = slice(part * G_WIDTH + cb * 2 * LANES, part * G_WIDTH + (cb + 1) * 2 * LANES)
            acc = conv_ref[CONV_WIDTH - 1:CONV_WIDTH, cols] * xbuf_scr[8:8 + TILE, cols]
            for j in range(CONV_WIDTH - 1):
                off = 8 - (CONV_WIDTH - 1) + j
                acc = acc + conv_ref[j:j + 1, cols] * xbuf_scr[off:off + TILE, cols]
            act = acc * jax.nn.sigmoid(acc)
            if part < 2:
                ssq = _dot((act * act).astype(BF16), ones_bd)
                act = act * lax.rsqrt(ssq + EPS)
            dst[:, cb * 2 * LANES:(cb + 1) * 2 * LANES] = act
    xbuf_scr[0:8, :] = xbuf_scr[TILE:TILE + 8, :]

    ri = lax.broadcasted_iota(jnp.int32, (TILE, TILE), 0)
    ci = lax.broadcasted_iota(jnp.int32, (TILE, TILE), 1)
    incl = ri >= ci

    def wide(x128):
        return jnp.concatenate([x128, x128], axis=1)

    ones_v = jnp.ones((TILE, LANES), BF16)
    ones_full = jnp.ones((M_V_DIM, M_V_DIM), BF16)
    for h in range(M_HEADS):
        q = pm_ref[:, OFF_MQ + h * M_QK_DIM:OFF_MQ + (h + 1) * M_QK_DIM]
        k = pm_ref[:, OFF_MK + h * M_QK_DIM:OFF_MK + (h + 1) * M_QK_DIM]
        v = pm_ref[:, OFF_MV + h * M_V_DIM:OFF_MV + (h + 1) * M_V_DIM]
        v_aug = jnp.concatenate([v, ones_v], axis=1)
        p = jnp.where(incl, jnp.exp(wide(bc_scr[h * M_BC + 0]) + row_form[h:h + 1, :]), 0.0) * _dot_nt(q, k)
        c_aug = caug_scr[h]
        w_in3 = bc_scr[h * M_BC + 1]
        nd = (jnp.concatenate([w_in3, w_in3, w_in3], axis=1) * _dot(q, c_aug.astype(BF16))
              + _dot(p.astype(BF16), v_aug))
        den = jnp.maximum(jnp.abs(nd[:, M_V_DIM:]), bc_scr[h * M_BC + 2])
        hm = nd[:, :M_V_DIM] * wide(1.0 / den)
        kw = (k.astype(F32) * bc_scr[h * M_BC + 3]).astype(BF16)
        dec = jnp.broadcast_to(decay_m[0:1, h:h + 1], (M_QK_DIM, M_AUG))
        caug_scr[h] = dec * c_aug + _dot_tn(kw, v_aug)
        cols = slice(h * M_V_DIM, (h + 1) * M_V_DIM)
        ms = _dot((hm * hm).astype(BF16), ones_full) * (1.0 / M_V_DIM)
        o_gate = jax.nn.sigmoid(pm_ref[:, OFF_MO + h * M_V_DIM:OFF_MO + (h + 1) * M_V_DIM].astype(F32))
        z = pm_ref[:, OFF_MZ + h * M_V_DIM:OFF_MZ + (h + 1) * M_V_DIM].astype(F32)
        y = hm * lax.rsqrt(ms + EPS) * mnorm_ref[0:1, cols] * o_gate * (z * jax.nn.sigmoid(z))
        mix_ref[:, cols] = y.astype(BF16)

    g_scale = G_DIM ** -0.5
    heads = range(G_HEADS)
    hs = lambda j: slice(j * G_DIM, (j + 1) * G_DIM)
    bc = lambda j, i: bc_scr[M_HEADS * M_BC + j * G_BC + i]
    k_b = [gk_scr[:, hs(j)].astype(BF16) for j in heads]
    q_b = [(gq_scr[:, hs(j)] * g_scale).astype(BF16) for j in heads]
    gam = [jnp.where(incl, jnp.exp(wide(bc(j, 0)) + row_form[M_HEADS + j:M_HEADS + j + 1, :]), 0.0)
           for j in heads]
    a_qk = [(_dot_nt(q_b[j], k_b[j]) * gam[j]).astype(BF16) for j in heads]
    low = [wide(bc(j, 1)) * _dot_nt(k_b[j], k_b[j]) * gam[j] for j in heads]
    dinv = [mask_ref[N_LEVELS] - low[j] * mask_ref[0] for j in heads]
    for lvl in range(1, N_LEVELS):
        d_b = [dinv[j].astype(BF16) for j in heads]
        e_b = [_dot((low[j] * mask_ref[lvl]).astype(BF16), d_b[j]).astype(BF16) for j in heads]
        dinv = [dinv[j] - _dot(d_b[j], e_b[j]) for j in heads]
    rhs = [jnp.concatenate([gv_scr[:, hs(j)] * bc(j, 1), gk_scr[:, hs(j)] * (bc(j, 1) * bc(j, 2))],
                           axis=1).astype(BF16) for j in heads]
    sol = [_dot(dinv[j].astype(BF16), rhs[j]) for j in heads]
    s_old = [s_scr[j] for j in heads]
    s_b = [s_old[j].astype(BF16) for j in heads]
    v_new = [(sol[j][:, :G_DIM] - _dot(sol[j][:, G_DIM:].astype(BF16), s_b[j])).astype(BF16) for j in heads]
    for j in heads:
        kd = (gk_scr[:, hs(j)] * bc(j, 3)).astype(BF16)
        dec = jnp.broadcast_to(e_last[0:1, M_HEADS + j:M_HEADS + j + 1], (G_DIM, G_DIM))
        s_scr[j] = dec * s_old[j] + _dot_tn(kd, v_new[j])
    og = [_dot((gq_scr[:, hs(j)] * (bc(j, 2) * g_scale)).astype(BF16), s_b[j]) + _dot(a_qk[j], v_new[j])
          for j in heads]
    for cb in range(G_HEADS // 2):
        cols = slice(cb * 2 * LANES, (cb + 1) * 2 * LANES)
        o2 = jnp.concatenate([og[2 * cb], og[2 * cb + 1]], axis=1)
        ms = _dot((o2 * o2).astype(BF16), ones_bd) * (1.0 / G_DIM)
        z = pm_ref[:, OFF_GZ + cb * 2 * LANES:OFF_GZ + (cb + 1) * 2 * LANES].astype(F32)
        y = o2 * lax.rsqrt(ms + EPS) * gnorm_ref[0:1, cols] * (z * jax.nn.sigmoid(z))
        mix_ref[:, M_WIDTH + cb * 2 * LANES:M_WIDTH + (cb + 1) * 2 * LANES] = y.astype(BF16)


def _mixers(pm, gt, pvec, conv_w, mnorm, gnorm, batch, seq):
    steps = seq // TILE
    const = lambda b, t: (0, 0)
    tok = lambda b, t: (b * steps + t, 0)
    return pl.pallas_call(
        _mixer_kernel,
        grid=(batch, steps),
        in_specs=[
            pl.BlockSpec((TILE, N_MAIN), tok),
            pl.BlockSpec((TILE, N_GATE), tok),
            pl.BlockSpec((8, LANES), const),
            pl.BlockSpec((CONV_WIDTH, 3 * G_WIDTH), const),
            pl.BlockSpec((1, M_WIDTH), const),
            pl.BlockSpec((1, G_WIDTH), const),
            pl.BlockSpec((N_LEVELS + 1, TILE, TILE), lambda b, t: (0, 0, 0), pipeline_mode=pl.Buffered(1)),
        ],
        out_specs=pl.BlockSpec((TILE, MIX_WIDTH), tok),
        out_shape=jax.ShapeDtypeStruct((batch * seq, MIX_WIDTH), BF16),
        scratch_shapes=[
            pltpu.VMEM((M_HEADS, M_QK_DIM, M_AUG), F32),
            pltpu.VMEM((1, LANES), F32),
            pltpu.VMEM((G_HEADS, G_DIM, G_DIM), F32),
            pltpu.VMEM((TILE + 8, 3 * G_WIDTH), F32),
            pltpu.VMEM((TILE, G_WIDTH), F32),
            pltpu.VMEM((TILE, G_WIDTH), F32),
            pltpu.VMEM((TILE, G_WIDTH), F32),
            pltpu.VMEM((M_HEADS * M_BC + G_HEADS * G_BC, TILE, LANES), F32),
        ],
        compiler_params=pltpu.CompilerParams(
            dimension_semantics=("arbitrary", "arbitrary"), vmem_limit_bytes=VMEM_LIMIT),
    )(pm, gt, pvec, conv_w, mnorm, gnorm, _level_masks())


def _out_kernel(mix_ref, x_ref, wo_ref, fn_ref, o_ref):
    y = x_ref[...] + _dot(mix_ref[...], wo_ref[...])
    ms = jnp.mean(y * y, axis=-1, keepdims=True)
    o_ref[...] = y * lax.rsqrt(ms + EPS) * fn_ref[...]


def _output_projection(mix, x2, w_out, final_gain):
    rows = x2.shape[0]
    tm = min(PROJ_ROWS, rows)
    const = lambda i: (0, 0)
    return pl.pallas_call(
        _out_kernel,
        grid=(rows // tm,),
        in_specs=[
            pl.BlockSpec((tm, MIX_WIDTH), lambda i: (i, 0)),
            pl.BlockSpec((tm, D_MODEL), lambda i: (i, 0)),
            pl.BlockSpec((MIX_WIDTH, D_MODEL), const),
            pl.BlockSpec((1, D_MODEL), const),
        ],
        out_specs=pl.BlockSpec((tm, D_MODEL), lambda i: (i, 0)),
        out_shape=jax.ShapeDtypeStruct((rows, D_MODEL), F32),
        compiler_params=pltpu.CompilerParams(
            dimension_semantics=("arbitrary",), vmem_limit_bytes=VMEM_LIMIT),
    )(mix, x2, w_out, final_gain)


def _pad_lanes(v):
    return jnp.pad(v.astype(F32), (0, LANES - v.shape[0]))


def kernel(x, attn_norm, w_in, m_i_bias, m_f_bias, m_out_norm, g_conv, g_a_log, g_dt_bias,
           g_out_norm, w_out, final_norm):
    batch, seq, _ = x.shape
    assert attn_norm.shape[0] == 1, "single-layer block"
    assert seq % TILE == 0
    w = w_in[0]
    splits = (M_QK_WIDTH, M_QK_WIDTH, M_WIDTH, M_WIDTH, M_WIDTH, M_HEADS, M_HEADS,
              3 * G_WIDTH, G_WIDTH, G_HEADS, G_HEADS)
    offs = [0]
    for s in splits:
        offs.append(offs[-1] + s)
    mq, mk, mv, mo, mz, mi, mf, gqkv, gz, gb, ga = (w[:, offs[i]:offs[i + 1]] for i in range(len(splits)))
    w_main = jnp.concatenate([mq, mk, mv, mo, mz, gqkv, gz], axis=1).astype(BF16)
    zpad = jnp.zeros((D_MODEL, LANES - M_HEADS - G_HEADS), w.dtype)
    w_gate = jnp.concatenate([mi, gb, zpad, mf, ga, zpad], axis=1).astype(BF16)
    zero4 = jnp.zeros((M_HEADS,), F32)
    pvec = jnp.stack([
        _pad_lanes(m_i_bias[0]),
        _pad_lanes(jnp.concatenate([m_f_bias[0].astype(F32), g_dt_bias[0].astype(F32)])),
        _pad_lanes(jnp.concatenate([zero4, g_a_log[0].astype(F32)])),
    ] + [jnp.zeros((LANES,), F32)] * 5)
    gnorm = jnp.tile(g_out_norm[0].astype(F32), G_HEADS)[None, :]

    x2 = x.reshape(batch * seq, D_MODEL)
    pm, gt = _input_projection(x2, attn_norm.astype(F32), w_main, w_gate)
    mix = _mixers(pm, gt, pvec, g_conv[0].astype(F32), m_out_norm.astype(F32), gnorm, batch, seq)
    out = _output_projection(mix, x2, w_out[0].astype(BF16), final_norm.astype(F32)[None, :])
    return out.reshape(batch, seq, D_MODEL)
```

```python
import functools

import jax
import jax.numpy as jnp
import numpy as np
from jax import lax
from jax.experimental import pallas as pl
from jax.experimental.pallas import tpu as pltpu

F32 = jnp.float32
BF16 = jnp.bfloat16

D_MODEL = 1024
M_HEADS = 4
M_QK_DIM = 128
M_V_DIM = 256
M_QK_WIDTH = M_HEADS * M_QK_DIM
M_WIDTH = M_HEADS * M_V_DIM
G_HEADS = 8
G_DIM = 128
G_WIDTH = G_HEADS * G_DIM
CONV_WIDTH = 4
MIX_WIDTH = M_WIDTH + G_WIDTH
EPS = 1e-6
LANES = 128

OFF_MQ = 0
OFF_MK = OFF_MQ + M_QK_WIDTH
OFF_MV = OFF_MK + M_QK_WIDTH
OFF_MO = OFF_MV + M_WIDTH
OFF_MZ = OFF_MO + M_WIDTH
OFF_GQKV = OFF_MZ + M_WIDTH
OFF_GZ = OFF_GQKV + 3 * G_WIDTH
N_MAIN = OFF_GZ + G_WIDTH
N_GATE = 2 * LANES
M_AUG = M_V_DIM + LANES

PROJ_COLS = 1024
TILE = 256
HALF = TILE // 2
N_LEVELS = 7
G_GROUP = 8
VMEM_LIMIT = 60 * 1024 * 1024

M_BC = 4
G_BC = 4


def _dot(a, b):
    return jnp.dot(a, b, preferred_element_type=F32)


def _dot_nt(a, b):
    return lax.dot_general(a, b, (((1,), (1,)), ((), ())), preferred_element_type=F32)


def _dot_tn(a, b):
    return lax.dot_general(a, b, (((0,), (0,)), ((), ())), preferred_element_type=F32)


def _scan_rows(x, row, combine, identity):
    shift = 1
    while shift < TILE:
        x = combine(x, jnp.where(row >= shift, pltpu.roll(x, shift, 0), identity))
        shift *= 2
    return x


def _softplus(x):
    return jnp.maximum(x, 0.0) + jnp.log1p(jnp.exp(-jnp.abs(x)))


def _wide(x128):
    return jnp.concatenate([x128, x128], axis=1)


def _level_masks():
    i = np.arange(HALF)[:, None]
    j = np.arange(HALF)[None, :]
    out = []
    for l in range(N_LEVELS):
        s = 1 << l
        out.append((i // (2 * s) == j // (2 * s)) & ((i // s) % 2 == 1) & ((j // s) % 2 == 0))
    out.append(i == j)
    half = np.stack(out).astype(np.float32)
    return jnp.asarray(np.concatenate([half, half], axis=2), dtype=BF16)


def _fold(full):
    return jnp.concatenate([full[:HALF, :HALF], full[HALF:, HALF:]], axis=1)


def _unfold(folded):
    zero = jnp.zeros((HALF, HALF), folded.dtype)
    return jnp.concatenate([jnp.concatenate([folded[:, :HALF], zero], axis=1),
                            jnp.concatenate([zero, folded[:, HALF:]], axis=1)], axis=0)


def _projection_pieces(x_ref, gain_ref, wm_ref, wg_ref, h_scr, pm_dst, gt_dst):
    x = x_ref[...]
    ms = jnp.mean(x * x, axis=-1, keepdims=True)
    h_scr[...] = (x * lax.rsqrt(ms + EPS) * gain_ref[...]).astype(BF16)

    def gate_piece():
        gt_dst[...] = _dot(h_scr[...], wg_ref[...])

    def main_piece(nb):
        cols = slice(nb * PROJ_COLS, (nb + 1) * PROJ_COLS)
        pm_dst[:, cols] = _dot(h_scr[...], wm_ref[:, cols]).astype(BF16)

    return [gate_piece] + [functools.partial(main_piece, nb) for nb in range(N_MAIN // PROJ_COLS)]


def _mix(pm_ref, gt_ref, pv_ref, conv_ref, mnorm_ref, gnorm_ref, mask_ref, mix_scr,
         caug_scr, m_scr, s_scr, xbuf_scr, gq_scr, gk_scr, gv_scr, bc_scr, side_work, recycle):
    def side(n=1):
        for _ in range(n):
            if side_work:
                side_work.pop(0)()

    side(2)
    lane = lax.broadcasted_iota(jnp.int32, (TILE, LANES), 1)
    row = lax.broadcasted_iota(jnp.int32, (TILE, LANES), 0)
    is_m = lane < M_HEADS
    a = gt_ref[:, 0:LANES] + pv_ref[0:1, :]
    b = gt_ref[:, LANES:2 * LANES] + pv_ref[1:2, :]
    log_f = -_softplus(-b)
    log_alpha = -jnp.exp(pv_ref[2:3, :]) * _softplus(b)
    ab = jnp.where(is_m, a, jax.nn.sigmoid(a))
    cum = _scan_rows(jnp.where(is_m, log_f, log_alpha), row, jnp.add, 0.0)
    side()
    m_prev = m_scr[...]
    cmax = _scan_rows(ab - cum, row, jnp.maximum, -jnp.inf)
    inter = cum + m_prev
    m_t = jnp.maximum(inter, cum + cmax)
    last = cum[TILE - 1:TILE, :]
    a_s = last - cum + ab
    m_new = jnp.maximum(last + m_prev, jnp.max(a_s, axis=0, keepdims=True))
    decay_m = jnp.exp(last + m_prev - m_new)
    e_last = jnp.exp(last)
    m_scr[...] = jnp.where(is_m[0:1, :], m_new, 0.0)
    row_form = jnp.where(is_m, ab - cum + np.float32(np.log(M_QK_DIM ** -0.5)), -cum).T

    m_cols = (cum - m_t, jnp.exp(inter - m_t) * (M_QK_DIM ** -0.5), jnp.exp(-m_t), jnp.exp(a_s - m_new))
    g_cols = (cum, ab, jnp.exp(cum), jnp.exp(last - cum))
    for h in range(M_HEADS):
        for i, colv in enumerate(m_cols):
            bc_scr[h * M_BC + i] = jnp.broadcast_to(colv[:, h:h + 1], (TILE, LANES))
    side()
    for j in range(G_HEADS):
        for i, colv in enumerate(g_cols):
            bc_scr[M_HEADS * M_BC + j * G_BC + i] = jnp.broadcast_to(
                colv[:, M_HEADS + j:M_HEADS + j + 1], (TILE, LANES))
    side()

    xbuf_scr[8:8 + TILE, :] = pm_ref[:, OFF_GQKV:OFF_GQKV + 3 * G_WIDTH].astype(F32)
    ones_bd = jnp.where(
        lax.broadcasted_iota(jnp.int32, (2 * LANES, 2 * LANES), 0) // LANES
        == lax.broadcasted_iota(jnp.int32, (2 * LANES, 2 * LANES), 1) // LANES, 1.0, 0.0).astype(BF16)
    for part, dst in enumerate((gq_scr, gk_scr, gv_scr)):
        for cb in range(G_WIDTH // (2 * LANES)):
            cols = slice(part * G_WIDTH + cb * 2 * LANES, part * G_WIDTH + (cb + 1) * 2 * LANES)
            acc = conv_ref[CONV_WIDTH - 1:CONV_WIDTH, cols] * xbuf_scr[8:8 + TILE, cols]
            for j in range(CONV_WIDTH - 1):
                off = 8 - (CONV_WIDTH - 1) + j
                acc = acc + conv_ref[j:j + 1, cols] * xbuf_scr[off:off + TILE, cols]
            act = acc * jax.nn.sigmoid(acc)
            if part < 2:
                ssq = _dot((act * act).astype(BF16), ones_bd)
                act = act * lax.rsqrt(ssq + EPS)
            dst[:, cb * 2 * LANES:(cb + 1) * 2 * LANES] = act
            if (part * (G_WIDTH // (2 * LANES)) + cb) % 3 == 0:
                side()
    xbuf_scr[0:8, :] = xbuf_scr[TILE:TILE + 8, :]
    side(len(side_work))
    recycle(OFF_GQKV, OFF_GZ)

    ri = lax.broadcasted_iota(jnp.int32, (TILE, TILE), 0)
    ci = lax.broadcasted_iota(jnp.int32, (TILE, TILE), 1)
    incl = ri >= ci

    ones_v = jnp.ones((TILE, LANES), BF16)
    ones_full = jnp.ones((M_V_DIM, M_V_DIM), BF16)
    mh = range(M_HEADS)
    mbc = lambda h, i: bc_scr[h * M_BC + i]
    q_m = {h: pm_ref[:, OFF_MQ + h * M_QK_DIM:OFF_MQ + (h + 1) * M_QK_DIM] for h in mh}
    k_m = {h: pm_ref[:, OFF_MK + h * M_QK_DIM:OFF_MK + (h + 1) * M_QK_DIM] for h in mh}
    v_aug = {h: jnp.concatenate([pm_ref[:, OFF_MV + h * M_V_DIM:OFF_MV + (h + 1) * M_V_DIM], ones_v], axis=1)
             for h in mh}
    p_m = {h: (jnp.where(incl, jnp.exp(_wide(mbc(h, 0)) + row_form[h:h + 1, :]), 0.0)
               * _dot_nt(q_m[h], k_m[h])).astype(BF16) for h in mh}
    c_aug = {h: caug_scr[h] for h in mh}
    nd = {h: (jnp.concatenate([mbc(h, 1)] * 3, axis=1) * _dot(q_m[h], c_aug[h].astype(BF16))
              + _dot(p_m[h], v_aug[h])) for h in mh}
    for h in mh:
        kw = (k_m[h].astype(F32) * mbc(h, 3)).astype(BF16)
        dec = jnp.broadcast_to(decay_m[0:1, h:h + 1], (M_QK_DIM, M_AUG))
        caug_scr[h] = dec * c_aug[h] + _dot_tn(kw, v_aug[h])
    hm = {h: nd[h][:, :M_V_DIM] * _wide(1.0 / jnp.maximum(jnp.abs(nd[h][:, M_V_DIM:]), mbc(h, 2))) for h in mh}
    ms_m = {h: _dot((hm[h] * hm[h]).astype(BF16), ones_full) * (1.0 / M_V_DIM) for h in mh}
    for h in mh:
        cols = slice(h * M_V_DIM, (h + 1) * M_V_DIM)
        o_gate = jax.nn.sigmoid(pm_ref[:, OFF_MO + h * M_V_DIM:OFF_MO + (h + 1) * M_V_DIM].astype(F32))
        z = pm_ref[:, OFF_MZ + h * M_V_DIM:OFF_MZ + (h + 1) * M_V_DIM].astype(F32)
        y = hm[h] * lax.rsqrt(ms_m[h] + EPS) * mnorm_ref[0:1, cols] * o_gate * (z * jax.nn.sigmoid(z))
        mix_scr[:, cols] = y.astype(BF16)

    recycle(0, OFF_GQKV)

    g_scale = G_DIM ** -0.5
    hs = lambda j: slice(j * G_DIM, (j + 1) * G_DIM)
    bc = lambda j, i: bc_scr[M_HEADS * M_BC + j * G_BC + i]
    top, bot = slice(0, HALF), slice(HALF, TILE)
    for grp in range(G_HEADS // G_GROUP):
        heads = range(grp * G_GROUP, (grp + 1) * G_GROUP)
        k_b = {j: gk_scr[:, hs(j)].astype(BF16) for j in heads}
        q_b = {j: (gq_scr[:, hs(j)] * g_scale).astype(BF16) for j in heads}
        gam = {j: jnp.where(incl, jnp.exp(_wide(bc(j, 0)) + row_form[M_HEADS + j:M_HEADS + j + 1, :]), 0.0)
               for j in heads}
        a_qk = {j: (_dot_nt(q_b[j], k_b[j]) * gam[j]).astype(BF16) for j in heads}
        low = {j: (_wide(bc(j, 1)) * _dot_nt(k_b[j], k_b[j]) * gam[j]).astype(BF16) for j in heads}
        low_f = {j: _fold(low[j]) for j in heads}
        dinv = {j: (mask_ref[N_LEVELS] - low_f[j] * mask_ref[0]).astype(F32) for j in heads}
        for lvl in range(1, N_LEVELS):
            d_b = {j: dinv[j].astype(BF16) for j in heads}
            e_b = {j: _dot(low_f[j] * mask_ref[lvl], _unfold(d_b[j])).astype(BF16) for j in heads}
            dinv = {j: dinv[j] - _dot(d_b[j], _unfold(e_b[j])) for j in heads}
        rhs = {j: jnp.concatenate([gv_scr[:, hs(j)] * bc(j, 1), gk_scr[:, hs(j)] * (bc(j, 1) * bc(j, 2))], axis=1)
               for j in heads}
        sol_top = {j: _dot(dinv[j][:, :HALF].astype(BF16), rhs[j][top].astype(BF16)) for j in heads}
        res_bot = {j: (rhs[j][bot] - _dot(low[j][bot, top], sol_top[j].astype(BF16))).astype(BF16) for j in heads}
        sol_bot = {j: _dot(dinv[j][:, HALF:].astype(BF16), res_bot[j]) for j in heads}
        sol = {j: jnp.concatenate([sol_top[j], sol_bot[j]], axis=0) for j in heads}
        s_old = {j: s_scr[j] for j in heads}
        s_b = {j: s_old[j].astype(BF16) for j in heads}
        v_new = {j: (sol[j][:, :G_DIM] - _dot(sol[j][:, G_DIM:].astype(BF16), s_b[j])).astype(BF16) for j in heads}
        for j in heads:
            kd = (gk_scr[:, hs(j)] * bc(j, 3)).astype(BF16)
            dec = jnp.broadcast_to(e_last[0:1, M_HEADS + j:M_HEADS + j + 1], (G_DIM, G_DIM))
            s_scr[j] = dec * s_old[j] + _dot_tn(kd, v_new[j])
        og = {j: _dot((gq_scr[:, hs(j)] * (bc(j, 2) * g_scale)).astype(BF16), s_b[j]) + _dot(a_qk[j], v_new[j])
              for j in heads}
        for cb in range(grp * G_GROUP // 2, (grp + 1) * G_GROUP // 2):
            cols = slice(cb * 2 * LANES, (cb + 1) * 2 * LANES)
            o2 = jnp.concatenate([og[2 * cb], og[2 * cb + 1]], axis=1)
            ms = _dot((o2 * o2).astype(BF16), ones_bd) * (1.0 / G_DIM)
            z = pm_ref[:, OFF_GZ + cb * 2 * LANES:OFF_GZ + (cb + 1) * 2 * LANES].astype(F32)
            y = o2 * lax.rsqrt(ms + EPS) * gnorm_ref[0:1, cols] * (z * jax.nn.sigmoid(z))
            mix_scr[:, M_WIDTH + cb * 2 * LANES:M_WIDTH + (cb + 1) * 2 * LANES] = y.astype(BF16)


def _layer_kernel(x_next_ref, x_cur_ref, again_ref, wm_ref, wg_ref, pv_ref, conv_ref, mnorm_ref, gnorm_ref,
                  mask_ref, wo_ref, fn_ref, out_ref,
                  pm_cur, gt_cur, pm_next, gt_next, h_scr, mix_scr,
                  caug_scr, m_scr, s_scr, xbuf_scr, gq_scr, gk_scr, gv_scr, bc_scr, *, steps):
    g = pl.program_id(0)

    @pl.when(g == 0)
    def _():
        pm_cur[...] = jnp.zeros_like(pm_cur)
        gt_cur[...] = jnp.zeros_like(gt_cur)

    @pl.when(jnp.logical_or(g == 0, lax.rem(g + steps - 1, steps) == 0))
    def _():
        caug_scr[...] = jnp.zeros_like(caug_scr)
        m_scr[...] = jnp.zeros_like(m_scr)
        s_scr[...] = jnp.zeros_like(s_scr)
        xbuf_scr[0:8, :] = jnp.zeros((8, 3 * G_WIDTH), F32)

    side_work = _projection_pieces(x_next_ref, again_ref, wm_ref, wg_ref, h_scr, pm_next, gt_next)

    def recycle(lo, hi):
        pm_cur[:, lo:hi] = pm_next[:, lo:hi]

    _mix(pm_cur, gt_cur, pv_ref, conv_ref, mnorm_ref, gnorm_ref, mask_ref, mix_scr,
         caug_scr, m_scr, s_scr, xbuf_scr, gq_scr, gk_scr, gv_scr, bc_scr, side_work, recycle)
    y = x_cur_ref[...] + _dot(mix_scr[...], wo_ref[...])
    ms = jnp.mean(y * y, axis=-1, keepdims=True)
    out_ref[...] = y * lax.rsqrt(ms + EPS) * fn_ref[...]

    recycle(OFF_GZ, N_MAIN)
    gt_cur[...] = gt_next[...]


def _layer(x2, again, w_main, w_gate, pvec, conv_w, mnorm, gnorm, w_out, fgain, steps):
    n_tiles = x2.shape[0] // TILE
    const = lambda g: (0, 0)
    single = dict(pipeline_mode=pl.Buffered(1))
    return pl.pallas_call(
        functools.partial(_layer_kernel, steps=steps),
        grid=(n_tiles + 1,),
        in_specs=[
            pl.BlockSpec((TILE, D_MODEL), lambda g: (jnp.minimum(g, n_tiles - 1), 0)),
            pl.BlockSpec((TILE, D_MODEL), lambda g: (jnp.maximum(g - 1, 0), 0)),
            pl.BlockSpec((1, D_MODEL), const),
            pl.BlockSpec((D_MODEL, N_MAIN), const, **single),
            pl.BlockSpec((D_MODEL, N_GATE), const, **single),
            pl.BlockSpec((8, LANES), const),
            pl.BlockSpec((CONV_WIDTH, 3 * G_WIDTH), const),
            pl.BlockSpec((1, M_WIDTH), const),
            pl.BlockSpec((1, G_WIDTH), const),
            pl.BlockSpec((N_LEVELS + 1, HALF, TILE), lambda g: (0, 0, 0), **single),
            pl.BlockSpec((MIX_WIDTH, D_MODEL), const, **single),
            pl.BlockSpec((1, D_MODEL), const),
        ],
        out_specs=pl.BlockSpec((TILE, D_MODEL), lambda g: (jnp.maximum(g - 1, 0), 0)),
        out_shape=jax.ShapeDtypeStruct(x2.shape, F32),
        scratch_shapes=[
            pltpu.VMEM((TILE, N_MAIN), BF16),
            pltpu.VMEM((TILE, N_GATE), F32),
            pltpu.VMEM((TILE, N_MAIN), BF16),
            pltpu.VMEM((TILE, N_GATE), F32),
            pltpu.VMEM((TILE, D_MODEL), BF16),
            pltpu.VMEM((TILE, MIX_WIDTH), BF16),
            pltpu.VMEM((M_HEADS, M_QK_DIM, M_AUG), F32),
            pltpu.VMEM((1, LANES), F32),
            pltpu.VMEM((G_HEADS, G_DIM, G_DIM), F32),
            pltpu.VMEM((TILE + 8, 3 * G_WIDTH), F32),
            pltpu.VMEM((TILE, G_WIDTH), F32),
            pltpu.VMEM((TILE, G_WIDTH), F32),
            pltpu.VMEM((TILE, G_WIDTH), F32),
            pltpu.VMEM((M_HEADS * M_BC + G_HEADS * G_BC, TILE, LANES), F32),
        ],
        compiler_params=pltpu.CompilerParams(
            dimension_semantics=("arbitrary",), vmem_limit_bytes=VMEM_LIMIT),
    )(x2, x2, again, w_main, w_gate, pvec, conv_w, mnorm, gnorm, _level_masks(), w_out, fgain)


def _pad_lanes(v):
    return jnp.pad(v.astype(F32), (0, LANES - v.shape[0]))


def kernel(x, attn_norm, w_in, m_i_bias, m_f_bias, m_out_norm, g_conv, g_a_log, g_dt_bias,
           g_out_norm, w_out, final_norm):
    batch, seq, _ = x.shape
    assert attn_norm.shape[0] == 1, "single-layer block"
    assert seq % TILE == 0
    w = w_in[0]
    splits = (M_QK_WIDTH, M_QK_WIDTH, M_WIDTH, M_WIDTH, M_WIDTH, M_HEADS, M_HEADS,
              3 * G_WIDTH, G_WIDTH, G_HEADS, G_HEADS)
    offs = [0]
    for s in splits:
        offs.append(offs[-1] + s)
    mq, mk, mv, mo, mz, mi, mf, gqkv, gz, gb, ga = (w[:, offs[i]:offs[i + 1]] for i in range(len(splits)))
    w_main = jnp.concatenate([mq, mk, mv, mo, mz, gqkv, gz], axis=1).astype(BF16)
    zpad = jnp.zeros((D_MODEL, LANES - M_HEADS - G_HEADS), w.dtype)
    w_gate = jnp.concatenate([mi, gb, zpad, mf, ga, zpad], axis=1).astype(BF16)
    zero4 = jnp.zeros((M_HEADS,), F32)
    pvec = jnp.stack([
        _pad_lanes(m_i_bias[0]),
        _pad_lanes(jnp.concatenate([m_f_bias[0].astype(F32), g_dt_bias[0].astype(F32)])),
        _pad_lanes(jnp.concatenate([zero4, g_a_log[0].astype(F32)])),
    ] + [jnp.zeros((LANES,), F32)] * 5)
    gnorm = jnp.tile(g_out_norm[0].astype(F32), G_HEADS)[None, :]

    x2 = x.reshape(batch * seq, D_MODEL)
    out = _layer(x2, attn_norm.astype(F32), w_main, w_gate, pvec, g_conv[0].astype(F32),
                 m_out_norm.astype(F32), gnorm, w_out[0].astype(BF16),
                 final_norm.astype(F32)[None, :], seq // TILE)
    return out.reshape(batch, seq, D_MODEL)
```

```python
import functools

import jax
import jax.numpy as jnp
import numpy as np
from jax import lax
from jax.experimental import pallas as pl
from jax.experimental.pallas import tpu as pltpu

F32 = jnp.float32
BF16 = jnp.bfloat16

D_MODEL = 1024
M_HEADS = 4
M_QK_DIM = 128
M_V_DIM = 256
M_QK_WIDTH = M_HEADS * M_QK_DIM
M_WIDTH = M_HEADS * M_V_DIM
G_HEADS = 8
G_DIM = 128
G_WIDTH = G_HEADS * G_DIM
CONV_WIDTH = 4
MIX_WIDTH = M_WIDTH + G_WIDTH
EPS = 1e-6
LANES = 128

OFF_MQ = 0
OFF_MK = OFF_MQ + M_QK_WIDTH
OFF_MV = OFF_MK + M_QK_WIDTH
OFF_MO = OFF_MV + M_WIDTH
OFF_MZ = OFF_MO + M_WIDTH
OFF_GQKV = OFF_MZ + M_WIDTH
OFF_GZ = OFF_GQKV + 3 * G_WIDTH
N_MAIN = OFF_GZ + G_WIDTH
N_GATE = 2 * LANES

PROJ_COLS = 1024
OUT_COLS = 512
TILE = 256
HALF = TILE // 2
N_LEVELS = 7
G_GROUP = 8
VMEM_LIMIT = 60 * 1024 * 1024

M_BC = 3
G_BC = 4


def _dot(a, b):
    return jnp.dot(a, b, preferred_element_type=F32)


def _dot_nt(a, b):
    return lax.dot_general(a, b, (((1,), (1,)), ((), ())), preferred_element_type=F32)


def _dot_tn(a, b):
    return lax.dot_general(a, b, (((0,), (0,)), ((), ())), preferred_element_type=F32)


def _scan_rows(x, row, combine, identity):
    shift = 1
    while shift < TILE:
        x = combine(x, jnp.where(row >= shift, pltpu.roll(x, shift, 0), identity))
        shift *= 2
    return x


def _softplus(x):
    return jnp.maximum(x, 0.0) + jnp.log1p(jnp.exp(-jnp.abs(x)))


def _sigmoid(x):
    return 0.5 + 0.5 * jnp.tanh(0.5 * x)


def _silu(x):
    half = 0.5 * x
    return half + half * jnp.tanh(half)


def _wide(x128):
    return jnp.concatenate([x128, x128], axis=1)


def _level_masks():
    i = np.arange(HALF)[:, None]
    j = np.arange(HALF)[None, :]
    out = []
    for l in range(N_LEVELS):
        s = 1 << l
        out.append((i // (2 * s) == j // (2 * s)) & ((i // s) % 2 == 1) & ((j // s) % 2 == 0))
    out.append(i == j)
    half = np.stack(out).astype(np.float32)
    return jnp.asarray(np.concatenate([half, half], axis=2), dtype=BF16)


def _fold(full):
    return jnp.concatenate([full[:HALF, :HALF], full[HALF:, HALF:]], axis=1)


def _unfold(folded):
    zero = jnp.zeros((HALF, HALF), folded.dtype)
    return jnp.concatenate([jnp.concatenate([folded[:, :HALF], zero], axis=1),
                            jnp.concatenate([zero, folded[:, HALF:]], axis=1)], axis=0)


def _projection_pieces(x_ref, gain_ref, wm_ref, wg_ref, h_scr, pm_dst, gt_dst):
    x = x_ref[...]
    ms = jnp.mean(x * x, axis=-1, keepdims=True)
    h_scr[...] = (x * lax.rsqrt(ms + EPS) * gain_ref[...]).astype(BF16)

    def gate_piece():
        gt_dst[...] = _dot(h_scr[...], wg_ref[...])

    def main_piece(nb):
        cols = slice(nb * PROJ_COLS, (nb + 1) * PROJ_COLS)
        pm_dst[:, cols] = _dot(h_scr[...], wm_ref[:, cols]).astype(BF16)

    order = [OFF_GQKV // PROJ_COLS + i for i in range(3 * G_WIDTH // PROJ_COLS)]
    order += [nb for nb in range(N_MAIN // PROJ_COLS) if nb not in order]
    return [gate_piece] + [functools.partial(main_piece, nb) for nb in order]


def _output_pieces(mix_scr, x_ref, wo_ref, fn_ref, y_scr, out_ref):
    def block_piece(nb):
        cols = slice(nb * OUT_COLS, (nb + 1) * OUT_COLS)
        y_scr[:, cols] = x_ref[:, cols] + _dot(mix_scr[...], wo_ref[:, cols])

    def norm_piece():
        y = y_scr[...]
        ms = jnp.mean(y * y, axis=-1, keepdims=True)
        out_ref[...] = y * lax.rsqrt(ms + EPS) * fn_ref[...]

    return [functools.partial(block_piece, nb) for nb in range(D_MODEL // OUT_COLS)] + [norm_piece]


def _mix(pm_ref, gt_ref, pv_ref, conv_ref, mnorm_ref, gnorm_ref, mask_ref, mix_scr,
         c_scr, n_scr, m_scr, s_scr, xbuf_scr, gq_scr, gk_scr, gv_scr, bc_scr, side_work, recycle):
    def side(n=1):
        for _ in range(n):
            if side_work:
                side_work.pop(0)()

    side()
    lane = lax.broadcasted_iota(jnp.int32, (TILE, LANES), 1)
    row = lax.broadcasted_iota(jnp.int32, (TILE, LANES), 0)
    is_m = lane < M_HEADS
    a = gt_ref[:, 0:LANES] + pv_ref[0:1, :]
    b = gt_ref[:, LANES:2 * LANES] + pv_ref[1:2, :]
    log_f = -_softplus(-b)
    log_alpha = -jnp.exp(pv_ref[2:3, :]) * _softplus(b)
    ab = jnp.where(is_m, a, _sigmoid(a))
    cum = _scan_rows(jnp.where(is_m, log_f, log_alpha), row, jnp.add, 0.0)
    side()
    m_prev = m_scr[...]
    cmax = _scan_rows(ab - cum, row, jnp.maximum, -jnp.inf)
    inter = cum + m_prev
    m_t = jnp.maximum(inter, cum + cmax)
    last = cum[TILE - 1:TILE, :]
    a_s = last - cum + ab
    m_new = jnp.maximum(last + m_prev, jnp.max(a_s, axis=0, keepdims=True))
    decay_m = jnp.exp(last + m_prev - m_new)
    e_last = jnp.exp(last)
    m_scr[...] = jnp.where(is_m[0:1, :], m_new, 0.0)
    row_form = jnp.where(is_m, ab - cum + np.float32(np.log(M_QK_DIM ** -0.5)), -cum).T

    w_inter = jnp.exp(inter - m_t) * (M_QK_DIM ** -0.5)
    e_neg_mt = jnp.exp(-m_t)
    m_cols = (cum - m_t, w_inter, jnp.exp(a_s - m_new))
    g_cols = (cum, ab, jnp.exp(cum), jnp.exp(last - cum))
    for h in range(M_HEADS):
        for i, colv in enumerate(m_cols):
            bc_scr[h * M_BC + i] = jnp.broadcast_to(colv[:, h:h + 1], (TILE, LANES))
        if h % 2 == 1:
            side()
    for j in range(G_HEADS):
        for i, colv in enumerate(g_cols):
            bc_scr[M_HEADS * M_BC + j * G_BC + i] = jnp.broadcast_to(
                colv[:, M_HEADS + j:M_HEADS + j + 1], (TILE, LANES))
        if j % 4 == 3:
            side()

    xbuf_scr[8:8 + TILE, :] = pm_ref[:, OFF_GQKV:OFF_GQKV + 3 * G_WIDTH].astype(F32)
    ones_bd = jnp.where(
        lax.broadcasted_iota(jnp.int32, (2 * LANES, 2 * LANES), 0) // LANES
        == lax.broadcasted_iota(jnp.int32, (2 * LANES, 2 * LANES), 1) // LANES, 1.0, 0.0).astype(BF16)
    for part, dst in enumerate((gq_scr, gk_scr, gv_scr)):
        for cb in range(G_WIDTH // (2 * LANES)):
            cols = slice(part * G_WIDTH + cb * 2 * LANES, part * G_WIDTH + (cb + 1) * 2 * LANES)
            acc = conv_ref[CONV_WIDTH - 1:CONV_WIDTH, cols] * xbuf_scr[8:8 + TILE, cols]
            for j in range(CONV_WIDTH - 1):
                off = 8 - (CONV_WIDTH - 1) + j
                acc = acc + conv_ref[j:j + 1, cols] * xbuf_scr[off:off + TILE, cols]
            act = _silu(acc)
            if part < 2:
                ssq = _dot((act * act).astype(BF16), ones_bd)
                act = act * lax.rsqrt(ssq + EPS)
            dst[:, cb * 2 * LANES:(cb + 1) * 2 * LANES] = act
            if (part * (G_WIDTH // (2 * LANES)) + cb) % 4 == 0:
                side()
    xbuf_scr[0:8, :] = xbuf_scr[TILE:TILE + 8, :]
    recycle(OFF_GQKV, OFF_GZ)

    ri = lax.broadcasted_iota(jnp.int32, (TILE, TILE), 0)
    ci = lax.broadcasted_iota(jnp.int32, (TILE, TILE), 1)
    incl = ri >= ci

    mh = range(M_HEADS)
    mbc = lambda h, i: bc_scr[h * M_BC + i]
    q_m = {h: pm_ref[:, OFF_MQ + h * M_QK_DIM:OFF_MQ + (h + 1) * M_QK_DIM] for h in mh}
    k_m = {h: pm_ref[:, OFF_MK + h * M_QK_DIM:OFF_MK + (h + 1) * M_QK_DIM] for h in mh}
    v_m = {h: pm_ref[:, OFF_MV + h * M_V_DIM:OFF_MV + (h + 1) * M_V_DIM] for h in mh}
    p_f = {h: jnp.where(incl, jnp.exp(_wide(mbc(h, 0)) + row_form[h:h + 1, :]), 0.0) * _dot_nt(q_m[h], k_m[h])
           for h in mh}
    p_sum = {h: jnp.sum(p_f[h], axis=-1, keepdims=True) for h in mh}
    c_old = {h: c_scr[h] for h in mh}
    n_old = {h: n_scr[h, 0:1, :] for h in mh}
    q_n = {h: jnp.sum(q_m[h].astype(F32) * n_old[h], axis=-1, keepdims=True) for h in mh}
    num = {h: _wide(mbc(h, 1)) * _dot(q_m[h], c_old[h].astype(BF16)) + _dot(p_f[h].astype(BF16), v_m[h])
           for h in mh}
    for h in mh:
        kw = k_m[h].astype(F32) * mbc(h, 2)
        c_scr[h] = decay_m[0:1, h:h + 1] * c_old[h] + _dot_tn(kw.astype(BF16), v_m[h])
        n_scr[h, 0:1, :] = decay_m[0:1, h:h + 1] * n_old[h] + jnp.sum(kw, axis=0, keepdims=True)
    den = {h: jnp.maximum(jnp.abs(w_inter[:, h:h + 1] * q_n[h] + p_sum[h]), e_neg_mt[:, h:h + 1]) for h in mh}
    hm = {h: num[h] * (1.0 / den[h]) for h in mh}
    ms_m = {h: jnp.sum(hm[h] * hm[h], axis=-1, keepdims=True) * (1.0 / M_V_DIM) for h in mh}
    for h in mh:
        cols = slice(h * M_V_DIM, (h + 1) * M_V_DIM)
        o_gate = _sigmoid(pm_ref[:, OFF_MO + h * M_V_DIM:OFF_MO + (h + 1) * M_V_DIM].astype(F32))
        z = pm_ref[:, OFF_MZ + h * M_V_DIM:OFF_MZ + (h + 1) * M_V_DIM].astype(F32)
        y = hm[h] * lax.rsqrt(ms_m[h] + EPS) * mnorm_ref[0:1, cols] * o_gate * _silu(z)
        mix_scr[:, cols] = y.astype(BF16)

    recycle(OFF_MQ, OFF_MO)
    side()

    g_scale = G_DIM ** -0.5
    hs = lambda j: slice(j * G_DIM, (j + 1) * G_DIM)
    bc = lambda j, i: bc_scr[M_HEADS * M_BC + j * G_BC + i]
    top, bot = slice(0, HALF), slice(HALF, TILE)
    for grp in range(G_HEADS // G_GROUP):
        heads = range(grp * G_GROUP, (grp + 1) * G_GROUP)
        k_b = {j: gk_scr[:, hs(j)].astype(BF16) for j in heads}
        q_b = {j: (gq_scr[:, hs(j)] * g_scale).astype(BF16) for j in heads}
        gam = {j: jnp.where(incl, jnp.exp(_wide(bc(j, 0)) + row_form[M_HEADS + j:M_HEADS + j + 1, :]), 0.0)
               for j in heads}
        a_qk = {j: (_dot_nt(q_b[j], k_b[j]) * gam[j]).astype(BF16) for j in heads}
        low = {j: (_wide(bc(j, 1)) * _dot_nt(k_b[j], k_b[j]) * gam[j]).astype(BF16) for j in heads}
        low_f = {j: _fold(low[j]) for j in heads}
        dinv = {j: (mask_ref[N_LEVELS] - low_f[j] * mask_ref[0]).astype(F32) for j in heads}
        for lvl in range(1, N_LEVELS):
            if lvl % 2 == 0:
                side()
            d_b = {j: dinv[j].astype(BF16) for j in heads}
            e_b = {j: _dot(low_f[j] * mask_ref[lvl], _unfold(d_b[j])).astype(BF16) for j in heads}
            dinv = {j: dinv[j] - _dot(d_b[j], _unfold(e_b[j])) for j in heads}
        rhs = {j: jnp.concatenate([gv_scr[:, hs(j)] * bc(j, 1), gk_scr[:, hs(j)] * (bc(j, 1) * bc(j, 2))], axis=1)
               for j in heads}
        sol_top = {j: _dot(dinv[j][:, :HALF].astype(BF16), rhs[j][top].astype(BF16)) for j in heads}
        res_bot = {j: (rhs[j][bot] - _dot(low[j][bot, top], sol_top[j].astype(BF16))).astype(BF16) for j in heads}
        sol_bot = {j: _dot(dinv[j][:, HALF:].astype(BF16), res_bot[j]) for j in heads}
        sol = {j: jnp.concatenate([sol_top[j], sol_bot[j]], axis=0) for j in heads}
        pairs = range(grp * G_GROUP // 2, (grp + 1) * G_GROUP // 2)
        s_old = {p: s_scr[p] for p in pairs}
        s_bd = {p: _unfold(s_old[p].astype(BF16)) for p in pairs}
        w_s2 = {p: _dot(jnp.concatenate([sol[2 * p][:, G_DIM:], sol[2 * p + 1][:, G_DIM:]], axis=1).astype(BF16),
                        s_bd[p]) for p in pairs}
        v_new = {j: (sol[j][:, :G_DIM] - w_s2[j // 2][:, (j % 2) * G_DIM:(j % 2 + 1) * G_DIM]).astype(BF16)
                 for j in heads}
        for p in pairs:
            upd = []
            for j in (2 * p, 2 * p + 1):
                kd = (gk_scr[:, hs(j)] * bc(j, 3)).astype(BF16)
                upd.append(e_last[0:1, M_HEADS + j:M_HEADS + j + 1] * s_old[p][:, (j % 2) * G_DIM:(j % 2 + 1) * G_DIM]
                           + _dot_tn(kd, v_new[j]))
            s_scr[p] = jnp.concatenate(upd, axis=1)
        q_s2 = {p: _dot(jnp.concatenate([gq_scr[:, hs(2 * p)] * (bc(2 * p, 2) * g_scale),
                                         gq_scr[:, hs(2 * p + 1)] * (bc(2 * p + 1, 2) * g_scale)],
                                        axis=1).astype(BF16), s_bd[p]) for p in pairs}
        og = {j: q_s2[j // 2][:, (j % 2) * G_DIM:(j % 2 + 1) * G_DIM] + _dot(a_qk[j], v_new[j]) for j in heads}
        ms_g = {j: jnp.sum(og[j] * og[j], axis=-1, keepdims=True) * (1.0 / G_DIM) for j in heads}
        for j in heads:
            z = pm_ref[:, OFF_GZ + j * G_DIM:OFF_GZ + (j + 1) * G_DIM].astype(F32)
            y = og[j] * lax.rsqrt(ms_g[j] + EPS) * gnorm_ref[0:1, hs(j)] * _silu(z)
            mix_scr[:, M_WIDTH + j * G_DIM:M_WIDTH + (j + 1) * G_DIM] = y.astype(BF16)
    side(len(side_work))


def _layer_kernel(x_next_ref, x_cur_ref, again_ref, wm_ref, wg_ref, pv_ref, conv_ref, mnorm_ref, gnorm_ref,
                  mask_ref, wo_ref, fn_ref, out_ref,
                  pm_cur, gt_cur, pm_next, gt_next, h_scr, mix_scr, y_scr,
                  c_scr, n_scr, m_scr, s_scr, xbuf_scr, gq_scr, gk_scr, gv_scr, bc_scr, *, steps):
    g = pl.program_id(0)

    @pl.when(g == 0)
    def _():
        pm_cur[...] = jnp.zeros_like(pm_cur)
        gt_cur[...] = jnp.zeros_like(gt_cur)
        mix_scr[...] = jnp.zeros_like(mix_scr)

    @pl.when(jnp.logical_or(g == 0, lax.rem(g + steps - 1, steps) == 0))
    def _():
        c_scr[...] = jnp.zeros_like(c_scr)
        n_scr[...] = jnp.zeros_like(n_scr)
        m_scr[...] = jnp.zeros_like(m_scr)
        s_scr[...] = jnp.zeros_like(s_scr)
        xbuf_scr[0:8, :] = jnp.zeros((8, 3 * G_WIDTH), F32)

    side_work = (_output_pieces(mix_scr, x_cur_ref, wo_ref, fn_ref, y_scr, out_ref)
                 + _projection_pieces(x_next_ref, again_ref, wm_ref, wg_ref, h_scr, pm_next, gt_next))

    def recycle(lo, hi):
        pm_cur[:, lo:hi] = pm_next[:, lo:hi]

    _mix(pm_cur, gt_cur, pv_ref, conv_ref, mnorm_ref, gnorm_ref, mask_ref, mix_scr,
         c_scr, n_scr, m_scr, s_scr, xbuf_scr, gq_scr, gk_scr, gv_scr, bc_scr, side_work, recycle)
    recycle(OFF_MO, OFF_GQKV)
    recycle(OFF_GZ, N_MAIN)
    gt_cur[...] = gt_next[...]


def _layer(x2, again, w_main, w_gate, pvec, conv_w, mnorm, gnorm, w_out, fgain, steps):
    n_tiles = x2.shape[0] // TILE
    const = lambda g: (0, 0)
    single = dict(pipeline_mode=pl.Buffered(1))
    return pl.pallas_call(
        functools.partial(_layer_kernel, steps=steps),
        grid=(n_tiles + 2,),
        in_specs=[
            pl.BlockSpec((TILE, D_MODEL), lambda g: (jnp.minimum(g, n_tiles - 1), 0)),
            pl.BlockSpec((TILE, D_MODEL), lambda g: (jnp.maximum(g - 2, 0), 0)),
            pl.BlockSpec((1, D_MODEL), const),
            pl.BlockSpec((D_MODEL, N_MAIN), const, **single),
            pl.BlockSpec((D_MODEL, N_GATE), const, **single),
            pl.BlockSpec((8, LANES), const),
            pl.BlockSpec((CONV_WIDTH, 3 * G_WIDTH), const),
            pl.BlockSpec((1, M_WIDTH), const),
            pl.BlockSpec((1, G_WIDTH), const),
            pl.BlockSpec((N_LEVELS + 1, HALF, TILE), lambda g: (0, 0, 0), **single),
            pl.BlockSpec((MIX_WIDTH, D_MODEL), const, **single),
            pl.BlockSpec((1, D_MODEL), const),
        ],
        out_specs=pl.BlockSpec((TILE, D_MODEL), lambda g: (jnp.maximum(g - 2, 0), 0)),
        out_shape=jax.ShapeDtypeStruct(x2.shape, F32),
        scratch_shapes=[
            pltpu.VMEM((TILE, N_MAIN), BF16),
            pltpu.VMEM((TILE, N_GATE), F32),
            pltpu.VMEM((TILE, N_MAIN), BF16),
            pltpu.VMEM((TILE, N_GATE), F32),
            pltpu.VMEM((TILE, D_MODEL), BF16),
            pltpu.VMEM((TILE, MIX_WIDTH), BF16),
            pltpu.VMEM((TILE, D_MODEL), F32),
            pltpu.VMEM((M_HEADS, M_QK_DIM, M_V_DIM), F32),
            pltpu.VMEM((M_HEADS, 8, LANES), F32),
            pltpu.VMEM((1, LANES), F32),
            pltpu.VMEM((G_HEADS // 2, G_DIM, 2 * G_DIM), F32),
            pltpu.VMEM((TILE + 8, 3 * G_WIDTH), F32),
            pltpu.VMEM((TILE, G_WIDTH), F32),
            pltpu.VMEM((TILE, G_WIDTH), F32),
            pltpu.VMEM((TILE, G_WIDTH), F32),
            pltpu.VMEM((M_HEADS * M_BC + G_HEADS * G_BC, TILE, LANES), F32),
        ],
        compiler_params=pltpu.CompilerParams(
            dimension_semantics=("arbitrary",), vmem_limit_bytes=VMEM_LIMIT),
    )(x2, x2, again, w_main, w_gate, pvec, conv_w, mnorm, gnorm, _level_masks(), w_out, fgain)


def _pad_lanes(v):
    return jnp.pad(v.astype(F32), (0, LANES - v.shape[0]))


def kernel(x, attn_norm, w_in, m_i_bias, m_f_bias, m_out_norm, g_conv, g_a_log, g_dt_bias,
           g_out_norm, w_out, final_norm):
    batch, seq, _ = x.shape
    assert attn_norm.shape[0] == 1, "single-layer block"
    assert seq % TILE == 0
    w = w_in[0]
    splits = (M_QK_WIDTH, M_QK_WIDTH, M_WIDTH, M_WIDTH, M_WIDTH, M_HEADS, M_HEADS,
              3 * G_WIDTH, G_WIDTH, G_HEADS, G_HEADS)
    offs = [0]
    for s in splits:
        offs.append(offs[-1] + s)
    mq, mk, mv, mo, mz, mi, mf, gqkv, gz, gb, ga = (w[:, offs[i]:offs[i + 1]] for i in range(len(splits)))
    w_main = jnp.concatenate([mq, mk, mv, mo, mz, gqkv, gz], axis=1).astype(BF16)
    zpad = jnp.zeros((D_MODEL, LANES - M_HEADS - G_HEADS), w.dtype)
    w_gate = jnp.concatenate([mi, gb, zpad, mf, ga, zpad], axis=1).astype(BF16)
    zero4 = jnp.zeros((M_HEADS,), F32)
    pvec = jnp.stack([
        _pad_lanes(m_i_bias[0]),
        _pad_lanes(jnp.concatenate([m_f_bias[0].astype(F32), g_dt_bias[0].astype(F32)])),
        _pad_lanes(jnp.concatenate([zero4, g_a_log[0].astype(F32)])),
    ] + [jnp.zeros((LANES,), F32)] * 5)
    gnorm = jnp.tile(g_out_norm[0].astype(F32), G_HEADS)[None, :]

    x2 = x.reshape(batch * seq, D_MODEL)
    out = _layer(x2, attn_norm.astype(F32), w_main, w_gate, pvec, g_conv[0].astype(F32),
                 m_out_norm.astype(F32), gnorm, w_out[0].astype(BF16),
                 final_norm.astype(F32)[None, :], seq // TILE)
    return out.reshape(batch, seq, D_MODEL)
```

```python
import functools

import jax
import jax.numpy as jnp
import numpy as np
from jax import lax
from jax.experimental import pallas as pl
from jax.experimental.pallas import tpu as pltpu

F32 = jnp.float32
BF16 = jnp.bfloat16

D_MODEL = 1024
M_HEADS = 4
M_QK_DIM = 128
M_V_DIM = 256
M_QK_WIDTH = M_HEADS * M_QK_DIM
M_WIDTH = M_HEADS * M_V_DIM
G_HEADS = 8
G_DIM = 128
G_WIDTH = G_HEADS * G_DIM
CONV_WIDTH = 4
MIX_WIDTH = M_WIDTH + G_WIDTH
EPS = 1e-6
LANES = 128

OFF_MQ = 0
OFF_MK = OFF_MQ + M_QK_WIDTH
OFF_MV = OFF_MK + M_QK_WIDTH
OFF_MO = OFF_MV + M_WIDTH
OFF_MZ = OFF_MO + M_WIDTH
OFF_GQKV = OFF_MZ + M_WIDTH
OFF_GZ = OFF_GQKV + 3 * G_WIDTH
N_MAIN = OFF_GZ + G_WIDTH
N_GATE = 2 * LANES

PROJ_COLS = 1024
OUT_COLS = 512
CONV_COLS = 2 * LANES
TILE = 256
HALF = TILE // 2
N_LEVELS = 7
VMEM_LIMIT = 60 * 1024 * 1024

M_BC = 3
G_BC = 4


def _dot(a, b):
    return jnp.dot(a, b, preferred_element_type=F32)


def _dot_nt(a, b):
    return lax.dot_general(a, b, (((1,), (1,)), ((), ())), preferred_element_type=F32)


def _dot_tn(a, b):
    return lax.dot_general(a, b, (((0,), (0,)), ((), ())), preferred_element_type=F32)


def _scan_rows(x, row, combine, identity):
    shift = 1
    while shift < TILE:
        x = combine(x, jnp.where(row >= shift, pltpu.roll(x, shift, 0), identity))
        shift *= 2
    return x


def _softplus(x):
    return jnp.maximum(x, 0.0) + jnp.log1p(jnp.exp(-jnp.abs(x)))


def _sigmoid(x):
    return 0.5 + 0.5 * jnp.tanh(0.5 * x)


def _silu(x):
    half = 0.5 * x
    return half + half * jnp.tanh(half)


def _wide(x128):
    return jnp.concatenate([x128, x128], axis=1)


def _level_masks():
    i = np.arange(HALF)[:, None]
    j = np.arange(HALF)[None, :]
    out = []
    for l in range(N_LEVELS):
        s = 1 << l
        out.append((i // (2 * s) == j // (2 * s)) & ((i // s) % 2 == 1) & ((j // s) % 2 == 0))
    out.append(i == j)
    half = np.stack(out).astype(np.float32)
    return jnp.asarray(np.concatenate([half, half], axis=2), dtype=BF16)


def _fold(full):
    return jnp.concatenate([full[:HALF, :HALF], full[HALF:, HALF:]], axis=1)


def _unfold(folded):
    zero = jnp.zeros((HALF, HALF), folded.dtype)
    return jnp.concatenate([jnp.concatenate([folded[:, :HALF], zero], axis=1),
                            jnp.concatenate([zero, folded[:, HALF:]], axis=1)], axis=0)


def _projection_pieces(x_ref, gain_ref, wm_ref, wg_ref, h_scr, pm_dst, gt_dst):
    x = x_ref[...]
    ms = jnp.mean(x * x, axis=-1, keepdims=True)
    h_scr[...] = (x * lax.rsqrt(ms + EPS) * gain_ref[...]).astype(BF16)

    def gate_piece():
        gt_dst[...] = _dot(h_scr[...], wg_ref[...])

    issued = set()

    def main_piece(nb):
        cols = slice(nb * PROJ_COLS, (nb + 1) * PROJ_COLS)
        pm_dst[:, cols] = _dot(h_scr[...], wm_ref[:, cols]).astype(BF16)
        issued.add(nb)

    return gate_piece, {nb: functools.partial(main_piece, nb) for nb in range(N_MAIN // PROJ_COLS)}, issued


def _output_pieces(mix_scr, x_ref, wo_ref, fn_ref, y_scr, out_ref):
    def block_piece(nb):
        cols = slice(nb * OUT_COLS, (nb + 1) * OUT_COLS)
        y_scr[:, cols] = x_ref[:, cols] + _dot(mix_scr[...], wo_ref[:, cols])

    def norm_piece():
        y = y_scr[...]
        ms = jnp.mean(y * y, axis=-1, keepdims=True)
        out_ref[...] = y * lax.rsqrt(ms + EPS) * fn_ref[...]

    return [functools.partial(block_piece, nb) for nb in range(D_MODEL // OUT_COLS)] + [norm_piece]


def _front_end_pieces(pm_src, conv_ref, xbuf_scr, dsts):
    def load_piece():
        xbuf_scr[8:8 + TILE, :] = pm_src[:, OFF_GQKV:OFF_GQKV + 3 * G_WIDTH].astype(F32)

    def block_piece(part, cb):
        ones_bd = jnp.where(
            lax.broadcasted_iota(jnp.int32, (CONV_COLS, CONV_COLS), 0) // LANES
            == lax.broadcasted_iota(jnp.int32, (CONV_COLS, CONV_COLS), 1) // LANES, 1.0, 0.0).astype(BF16)
        cols = slice(part * G_WIDTH + cb * CONV_COLS, part * G_WIDTH + (cb + 1) * CONV_COLS)
        acc = conv_ref[CONV_WIDTH - 1:CONV_WIDTH, cols] * xbuf_scr[8:8 + TILE, cols]
        for j in range(CONV_WIDTH - 1):
            off = 8 - (CONV_WIDTH - 1) + j
            acc = acc + conv_ref[j:j + 1, cols] * xbuf_scr[off:off + TILE, cols]
        act = _silu(acc)
        if part < 2:
            act = act * lax.rsqrt(_dot((act * act).astype(BF16), ones_bd) + EPS)
        dsts[part][:, cb * CONV_COLS:(cb + 1) * CONV_COLS] = act

    def tail_piece():
        xbuf_scr[0:8, :] = xbuf_scr[TILE:TILE + 8, :]

    blocks = [functools.partial(block_piece, part, cb)
              for part in range(3) for cb in range(G_WIDTH // CONV_COLS)]
    return [load_piece] + blocks + [tail_piece]


def _mix(pm_ref, gt_ref, pv_ref, mnorm_ref, gnorm_ref, mask_ref, mix_scr,
         c_scr, n_scr, m_scr, s_scr, gq_scr, gk_scr, gv_scr, bc_scr, side_work, recycle):
    def side(n=1):
        for _ in range(n):
            if side_work:
                side_work.pop(0)()

    side()
    lane = lax.broadcasted_iota(jnp.int32, (TILE, LANES), 1)
    row = lax.broadcasted_iota(jnp.int32, (TILE, LANES), 0)
    is_m = lane < M_HEADS
    a = gt_ref[:, 0:LANES] + pv_ref[0:1, :]
    b = gt_ref[:, LANES:2 * LANES] + pv_ref[1:2, :]
    log_f = -_softplus(-b)
    log_alpha = -jnp.exp(pv_ref[2:3, :]) * _softplus(b)
    ab = jnp.where(is_m, a, _sigmoid(a))
    cum = _scan_rows(jnp.where(is_m, log_f, log_alpha), row, jnp.add, 0.0)
    side()
    m_prev = m_scr[...]
    cmax = _scan_rows(ab - cum, row, jnp.maximum, -jnp.inf)
    inter = cum + m_prev
    m_t = jnp.maximum(inter, cum + cmax)
    last = cum[TILE - 1:TILE, :]
    a_s = last - cum + ab
    m_new = jnp.maximum(last + m_prev, jnp.max(a_s, axis=0, keepdims=True))
    decay_m = jnp.exp(last + m_prev - m_new)
    e_last = jnp.exp(last)
    m_scr[...] = jnp.where(is_m[0:1, :], m_new, 0.0)
    row_form = jnp.where(is_m, ab - cum + np.float32(np.log(M_QK_DIM ** -0.5)), -cum).T

    w_inter = jnp.exp(inter - m_t) * (M_QK_DIM ** -0.5)
    e_neg_mt = jnp.exp(-m_t)
    m_cols = (cum - m_t, w_inter, jnp.exp(a_s - m_new))
    g_cols = (cum, ab, jnp.exp(cum), jnp.exp(last - cum))
    for h in range(M_HEADS):
        for i, colv in enumerate(m_cols):
            bc_scr[h * M_BC + i] = jnp.broadcast_to(colv[:, h:h + 1], (TILE, LANES))
        if h % 2 == 1:
            side()
    for j in range(G_HEADS):
        for i, colv in enumerate(g_cols):
            bc_scr[M_HEADS * M_BC + j * G_BC + i] = jnp.broadcast_to(
                colv[:, M_HEADS + j:M_HEADS + j + 1], (TILE, LANES))
        if j % 2 == 1:
            side()

    ri = lax.broadcasted_iota(jnp.int32, (TILE, TILE), 0)
    ci = lax.broadcasted_iota(jnp.int32, (TILE, TILE), 1)
    incl = ri >= ci

    mh = range(M_HEADS)
    mbc = lambda h, i: bc_scr[h * M_BC + i]
    q_m = {h: pm_ref[:, OFF_MQ + h * M_QK_DIM:OFF_MQ + (h + 1) * M_QK_DIM] for h in mh}
    k_m = {h: pm_ref[:, OFF_MK + h * M_QK_DIM:OFF_MK + (h + 1) * M_QK_DIM] for h in mh}
    v_m = {h: pm_ref[:, OFF_MV + h * M_V_DIM:OFF_MV + (h + 1) * M_V_DIM] for h in mh}
    p_f = {}
    for h in mh:
        p_f[h] = (jnp.where(incl, jnp.exp(_wide(mbc(h, 0)) + row_form[h:h + 1, :]), 0.0)
                  * _dot_nt(q_m[h], k_m[h]))
        if h % 2 == 1:
            side()
    p_sum = {h: jnp.sum(p_f[h], axis=-1, keepdims=True) for h in mh}
    c_old = {h: c_scr[h] for h in mh}
    n_old = {h: n_scr[h, 0:1, :] for h in mh}
    q_n = {h: jnp.sum(q_m[h].astype(F32) * n_old[h], axis=-1, keepdims=True) for h in mh}
    side()
    num = {}
    for h in mh:
        num[h] = _wide(mbc(h, 1)) * _dot(q_m[h], c_old[h].astype(BF16)) + _dot(p_f[h].astype(BF16), v_m[h])
        if h % 2 == 1:
            side()
    for h in mh:
        kw = k_m[h].astype(F32) * mbc(h, 2)
        c_scr[h] = decay_m[0:1, h:h + 1] * c_old[h] + _dot_tn(kw.astype(BF16), v_m[h])
        n_scr[h, 0:1, :] = decay_m[0:1, h:h + 1] * n_old[h] + jnp.sum(kw, axis=0, keepdims=True)
    side()
    den = {h: jnp.maximum(jnp.abs(w_inter[:, h:h + 1] * q_n[h] + p_sum[h]), e_neg_mt[:, h:h + 1]) for h in mh}
    hm = {h: num[h] * (1.0 / den[h]) for h in mh}
    ms_m = {h: jnp.sum(hm[h] * hm[h], axis=-1, keepdims=True) * (1.0 / M_V_DIM) for h in mh}
    side()
    for h in mh:
        cols = slice(h * M_V_DIM, (h + 1) * M_V_DIM)
        o_gate = _sigmoid(pm_ref[:, OFF_MO + h * M_V_DIM:OFF_MO + (h + 1) * M_V_DIM].astype(F32))
        z = pm_ref[:, OFF_MZ + h * M_V_DIM:OFF_MZ + (h + 1) * M_V_DIM].astype(F32)
        y = hm[h] * lax.rsqrt(ms_m[h] + EPS) * mnorm_ref[0:1, cols] * o_gate * _silu(z)
        mix_scr[:, cols] = y.astype(BF16)
        if h % 2 == 1:
            side()

    g_scale = G_DIM ** -0.5
    heads = range(G_HEADS)
    pairs = range(G_HEADS // 2)
    hs = lambda j: slice(j * G_DIM, (j + 1) * G_DIM)
    half_of = lambda j: slice((j % 2) * G_DIM, (j % 2 + 1) * G_DIM)
    bc = lambda j, i: bc_scr[M_HEADS * M_BC + j * G_BC + i]
    top, bot = slice(0, HALF), slice(HALF, TILE)
    k_b = {j: gk_scr[:, hs(j)].astype(BF16) for j in heads}
    q_b = {j: (gq_scr[:, hs(j)] * g_scale).astype(BF16) for j in heads}
    side()
    gam, a_qk, low = {}, {}, {}
    for j in heads:
        gam[j] = jnp.where(incl, jnp.exp(_wide(bc(j, 0)) + row_form[M_HEADS + j:M_HEADS + j + 1, :]), 0.0)
        a_qk[j] = (_dot_nt(q_b[j], k_b[j]) * gam[j]).astype(BF16)
        low[j] = (_wide(bc(j, 1)) * _dot_nt(k_b[j], k_b[j]) * gam[j]).astype(BF16)
        if j % 2 == 1:
            side()
    low_f = {j: _fold(low[j]) for j in heads}
    dinv = {j: (mask_ref[N_LEVELS] - low_f[j] * mask_ref[0]).astype(F32) for j in heads}
    for lvl in range(1, N_LEVELS):
        d_b = {j: dinv[j].astype(BF16) for j in heads}
        e_b = {j: _dot(low_f[j] * mask_ref[lvl], _unfold(d_b[j])).astype(BF16) for j in heads}
        side()
        dinv = {j: dinv[j] - _dot(d_b[j], _unfold(e_b[j])) for j in heads}
        side()
    recycle(OFF_MQ, OFF_GQKV)
    rhs = {j: jnp.concatenate([gv_scr[:, hs(j)] * bc(j, 1), gk_scr[:, hs(j)] * (bc(j, 1) * bc(j, 2))], axis=1)
           for j in heads}
    sol_top = {j: _dot(dinv[j][:, :HALF].astype(BF16), rhs[j][top].astype(BF16)) for j in heads}
    side()
    res_bot = {j: (rhs[j][bot] - _dot(low[j][bot, top], sol_top[j].astype(BF16))).astype(BF16) for j in heads}
    sol_bot = {j: _dot(dinv[j][:, HALF:].astype(BF16), res_bot[j]) for j in heads}
    side()
    sol = {j: jnp.concatenate([sol_top[j], sol_bot[j]], axis=0) for j in heads}
    s_old = {p: s_scr[p] for p in pairs}
    s_bd = {p: _unfold(s_old[p].astype(BF16)) for p in pairs}
    w_s2 = {p: _dot(jnp.concatenate([sol[2 * p][:, G_DIM:], sol[2 * p + 1][:, G_DIM:]], axis=1).astype(BF16),
                    s_bd[p]) for p in pairs}
    v_new = {j: (sol[j][:, :G_DIM] - w_s2[j // 2][:, half_of(j)]).astype(BF16) for j in heads}
    side()
    for p in pairs:
        upd = []
        for j in (2 * p, 2 * p + 1):
            kd = (gk_scr[:, hs(j)] * bc(j, 3)).astype(BF16)
            upd.append(e_last[0:1, M_HEADS + j:M_HEADS + j + 1] * s_old[p][:, half_of(j)] + _dot_tn(kd, v_new[j]))
        s_scr[p] = jnp.concatenate(upd, axis=1)
    side()
    q_s2 = {p: _dot(jnp.concatenate([gq_scr[:, hs(2 * p)] * (bc(2 * p, 2) * g_scale),
                                     gq_scr[:, hs(2 * p + 1)] * (bc(2 * p + 1, 2) * g_scale)],
                                    axis=1).astype(BF16), s_bd[p]) for p in pairs}
    og = {j: q_s2[j // 2][:, half_of(j)] + _dot(a_qk[j], v_new[j]) for j in heads}
    side()
    ms_g = {j: jnp.sum(og[j] * og[j], axis=-1, keepdims=True) * (1.0 / G_DIM) for j in heads}
    for j in heads:
        z = pm_ref[:, OFF_GZ + j * G_DIM:OFF_GZ + (j + 1) * G_DIM].astype(F32)
        y = og[j] * lax.rsqrt(ms_g[j] + EPS) * gnorm_ref[0:1, hs(j)] * _silu(z)
        mix_scr[:, M_WIDTH + j * G_DIM:M_WIDTH + (j + 1) * G_DIM] = y.astype(BF16)
        if j % 2 == 1:
            side()
    side(len(side_work))


def _layer_kernel(x_next_ref, x_cur_ref, again_ref, wm_ref, wg_ref, pv_ref, conv_ref, mnorm_ref, gnorm_ref,
                  mask_ref, wo_ref, fn_ref, out_ref,
                  pm_cur, gt_cur, pm_next, gt_next, h_scr, mix_scr, y_scr,
                  c_scr, n_scr, m_scr, s_scr, xbuf_scr,
                  gq_cur, gk_cur, gv_cur, gq_next, gk_next, gv_next, bc_scr, *, steps):
    g = pl.program_id(0)

    @pl.when(g == 0)
    def _():
        pm_cur[...] = jnp.zeros_like(pm_cur)
        gt_cur[...] = jnp.zeros_like(gt_cur)
        mix_scr[...] = jnp.zeros_like(mix_scr)
        gq_cur[...] = jnp.zeros_like(gq_cur)
        gk_cur[...] = jnp.zeros_like(gk_cur)
        gv_cur[...] = jnp.zeros_like(gv_cur)

    @pl.when(jnp.logical_or(g == 0, lax.rem(g + steps - 1, steps) == 0))
    def _():
        c_scr[...] = jnp.zeros_like(c_scr)
        n_scr[...] = jnp.zeros_like(n_scr)
        m_scr[...] = jnp.zeros_like(m_scr)
        s_scr[...] = jnp.zeros_like(s_scr)

    @pl.when(lax.rem(g, steps) == 0)
    def _():
        xbuf_scr[0:8, :] = jnp.zeros((8, 3 * G_WIDTH), F32)

    finish = _output_pieces(mix_scr, x_cur_ref, wo_ref, fn_ref, y_scr, out_ref)
    gate_piece, proj, projected = _projection_pieces(x_next_ref, again_ref, wm_ref, wg_ref, h_scr, pm_next, gt_next)
    front = _front_end_pieces(pm_next, conv_ref, xbuf_scr, (gq_next, gk_next, gv_next))
    qkv_blocks = [OFF_GQKV // PROJ_COLS + i for i in range(3 * G_WIDTH // PROJ_COLS)]
    rest = [proj[nb] for nb in sorted(proj) if nb not in qkv_blocks]
    load_piece, conv = front[0], front[1:]
    skip = lambda: None
    side_work = (finish + [gate_piece] + [proj[nb] for nb in qkv_blocks] + [load_piece]
                 + [rest[0], skip, rest[1], skip, skip, rest[2], skip, skip, skip]
                 + [rest[3], conv[0], skip, conv[1], skip]
                 + conv[2:5] + [rest[4]] + conv[5:])
    assert len(rest) == 5

    def recycle(lo, hi):
        assert all(nb in projected for nb in range(lo // PROJ_COLS, hi // PROJ_COLS)), "projection piece not issued yet"
        pm_cur[:, lo:hi] = pm_next[:, lo:hi]

    _mix(pm_cur, gt_cur, pv_ref, mnorm_ref, gnorm_ref, mask_ref, mix_scr,
         c_scr, n_scr, m_scr, s_scr, gq_cur, gk_cur, gv_cur, bc_scr, side_work, recycle)

    recycle(OFF_GZ, N_MAIN)
    gt_cur[...] = gt_next[...]
    gq_cur[...] = gq_next[...]
    gk_cur[...] = gk_next[...]
    gv_cur[...] = gv_next[...]


def _layer(x2, again, w_main, w_gate, pvec, conv_w, mnorm, gnorm, w_out, fgain, steps):
    n_tiles = x2.shape[0] // TILE
    const = lambda g: (0, 0)
    single = dict(pipeline_mode=pl.Buffered(1))
    return pl.pallas_call(
        functools.partial(_layer_kernel, steps=steps),
        grid=(n_tiles + 2,),
        in_specs=[
            pl.BlockSpec((TILE, D_MODEL), lambda g: (jnp.minimum(g, n_tiles - 1), 0)),
            pl.BlockSpec((TILE, D_MODEL), lambda g: (jnp.maximum(g - 2, 0), 0)),
            pl.BlockSpec((1, D_MODEL), const),
            pl.BlockSpec((D_MODEL, N_MAIN), const, **single),
            pl.BlockSpec((D_MODEL, N_GATE), const, **single),
            pl.BlockSpec((8, LANES), const),
            pl.BlockSpec((CONV_WIDTH, 3 * G_WIDTH), const),
            pl.BlockSpec((1, M_WIDTH), const),
            pl.BlockSpec((1, G_WIDTH), const),
            pl.BlockSpec((N_LEVELS + 1, HALF, TILE), lambda g: (0, 0, 0), **single),
            pl.BlockSpec((MIX_WIDTH, D_MODEL), const, **single),
            pl.BlockSpec((1, D_MODEL), const),
        ],
        out_specs=pl.BlockSpec((TILE, D_MODEL), lambda g: (jnp.maximum(g - 2, 0), 0)),
        out_shape=jax.ShapeDtypeStruct(x2.shape, F32),
        scratch_shapes=[
            pltpu.VMEM((TILE, N_MAIN), BF16),
            pltpu.VMEM((TILE, N_GATE), F32),
            pltpu.VMEM((TILE, N_MAIN), BF16),
            pltpu.VMEM((TILE, N_GATE), F32),
            pltpu.VMEM((TILE, D_MODEL), BF16),
            pltpu.VMEM((TILE, MIX_WIDTH), BF16),
            pltpu.VMEM((TILE, D_MODEL), F32),
            pltpu.VMEM((M_HEADS, M_QK_DIM, M_V_DIM), F32),
            pltpu.VMEM((M_HEADS, 8, LANES), F32),
            pltpu.VMEM((1, LANES), F32),
            pltpu.VMEM((G_HEADS // 2, G_DIM, 2 * G_DIM), F32),
            pltpu.VMEM((TILE + 8, 3 * G_WIDTH), F32),
            pltpu.VMEM((TILE, G_WIDTH), F32),
            pltpu.VMEM((TILE, G_WIDTH), F32),
            pltpu.VMEM((TILE, G_WIDTH), F32),
            pltpu.VMEM((TILE, G_WIDTH), F32),
            pltpu.VMEM((TILE, G_WIDTH), F32),
            pltpu.VMEM((TILE, G_WIDTH), F32),
            pltpu.VMEM((M_HEADS * M_BC + G_HEADS * G_BC, TILE, LANES), F32),
        ],
        compiler_params=pltpu.CompilerParams(
            dimension_semantics=("arbitrary",), vmem_limit_bytes=VMEM_LIMIT),
    )(x2, x2, again, w_main, w_gate, pvec, conv_w, mnorm, gnorm, _level_masks(), w_out, fgain)


def _pad_lanes(v):
    return jnp.pad(v.astype(F32), (0, LANES - v.shape[0]))


def kernel(x, attn_norm, w_in, m_i_bias, m_f_bias, m_out_norm, g_conv, g_a_log, g_dt_bias,
           g_out_norm, w_out, final_norm):
    batch, seq, _ = x.shape
    assert attn_norm.shape[0] == 1, "single-layer block"
    assert seq % TILE == 0
    w = w_in[0].astype(BF16)
    splits = (M_QK_WIDTH, M_QK_WIDTH, M_WIDTH, M_WIDTH, M_WIDTH, M_HEADS, M_HEADS,
              3 * G_WIDTH, G_WIDTH, G_HEADS, G_HEADS)
    offs = [0]
    for s in splits:
        offs.append(offs[-1] + s)
    mq, mk, mv, mo, mz, mi, mf, gqkv, gz, gb, ga = (w[:, offs[i]:offs[i + 1]] for i in range(len(splits)))
    w_main = jnp.concatenate([mq, mk, mv, mo, mz, gqkv, gz], axis=1)
    zpad = jnp.zeros((D_MODEL, LANES - M_HEADS - G_HEADS), BF16)
    w_gate = jnp.concatenate([mi, gb, zpad, mf, ga, zpad], axis=1)
    zero4 = jnp.zeros((M_HEADS,), F32)
    pvec = jnp.stack([
        _pad_lanes(m_i_bias[0]),
        _pad_lanes(jnp.concatenate([m_f_bias[0].astype(F32), g_dt_bias[0].astype(F32)])),
        _pad_lanes(jnp.concatenate([zero4, g_a_log[0].astype(F32)])),
    ] + [jnp.zeros((LANES,), F32)] * 5)
    gnorm = jnp.tile(g_out_norm[0].astype(F32), G_HEADS)[None, :]

    x2 = x.reshape(batch * seq, D_MODEL)
    out = _layer(x2, attn_norm.astype(F32), w_main, w_gate, pvec, g_conv[0].astype(F32),
                 m_out_norm.astype(F32), gnorm, w_out[0].astype(BF16),
                 final_norm.astype(F32)[None, :], seq // TILE)
    return out.reshape(batch, seq, D_MODEL)
```

```python
import functools

import jax
import jax.numpy as jnp
import numpy as np
from jax import lax
from jax.experimental import pallas as pl
from jax.experimental.pallas import tpu as pltpu

F32 = jnp.float32
BF16 = jnp.bfloat16

D_MODEL = 1024
M_HEADS = 4
M_QK_DIM = 128
M_V_DIM = 256
M_QK_WIDTH = M_HEADS * M_QK_DIM
M_WIDTH = M_HEADS * M_V_DIM
G_HEADS = 8
G_DIM = 128
G_WIDTH = G_HEADS * G_DIM
CONV_WIDTH = 4
MIX_WIDTH = M_WIDTH + G_WIDTH
EPS = 1e-6
LANES = 128

OFF_MQ = 0
OFF_MK = OFF_MQ + M_QK_WIDTH
OFF_MV = OFF_MK + M_QK_WIDTH
OFF_MO = OFF_MV + M_WIDTH
OFF_MZ = OFF_MO + M_WIDTH
OFF_GQKV = OFF_MZ + M_WIDTH
OFF_GZ = OFF_GQKV + 3 * G_WIDTH
N_MAIN = OFF_GZ + G_WIDTH
N_GATE = 2 * LANES

PROJ_COLS = 1024
OUT_COLS = 512
CONV_COLS = 2 * LANES
TILE = 256
HALF = TILE // 2
N_LEVELS = 7
ROW_GROUP = 16
VMEM_LIMIT = 60 * 1024 * 1024

M_BC = 3
G_BC = 4


def _dot(a, b):
    return jnp.dot(a, b, preferred_element_type=F32)


def _dot_nt(a, b):
    return lax.dot_general(a, b, (((1,), (1,)), ((), ())), preferred_element_type=F32)


def _dot_tn(a, b):
    return lax.dot_general(a, b, (((0,), (0,)), ((), ())), preferred_element_type=F32)


def _scan_rows(x, row, combine, identity):
    shift = 1
    while shift < TILE:
        x = combine(x, jnp.where(row >= shift, pltpu.roll(x, shift, 0), identity))
        shift *= 2
    return x


def _softplus(x):
    return jnp.maximum(x, 0.0) + jnp.log1p(jnp.exp(-jnp.abs(x)))


def _sigmoid(x):
    return 0.5 + 0.5 * jnp.tanh(0.5 * x)


def _silu(x):
    half = 0.5 * x
    return half + half * jnp.tanh(half)


def _wide(x128):
    return jnp.concatenate([x128, x128], axis=1)


def _level_masks():
    i = np.arange(HALF)[:, None]
    j = np.arange(HALF)[None, :]
    out = []
    for l in range(N_LEVELS):
        s = 1 << l
        out.append((i // (2 * s) == j // (2 * s)) & ((i // s) % 2 == 1) & ((j // s) % 2 == 0))
    out.append(i == j)
    half = np.stack(out).astype(np.float32)
    return jnp.asarray(np.concatenate([half, half], axis=2), dtype=BF16)


def _fold(full):
    return jnp.concatenate([full[:HALF, :HALF], full[HALF:, HALF:]], axis=1)


def _unfold(folded):
    zero = jnp.zeros((HALF, HALF), folded.dtype)
    return jnp.concatenate([jnp.concatenate([folded[:, :HALF], zero], axis=1),
                            jnp.concatenate([zero, folded[:, HALF:]], axis=1)], axis=0)


def _odd_blocks(x, s):
    return jnp.concatenate([x[r:r + s] for r in range(s, HALF, 2 * s)], axis=0)


def _spread_odd_blocks(x, s):
    zero = jnp.zeros((s, x.shape[1]), x.dtype)
    parts = []
    for i in range(HALF // (2 * s)):
        parts += [zero, x[i * s:(i + 1) * s]]
    return jnp.concatenate(parts, axis=0)


def _projection_pieces(x_ref, gain_ref, wa_ref, wb_ref, wg_ref, h_scr, pm_dst, gt_dst):
    x = x_ref[...]
    ms = jnp.mean(x * x, axis=-1, keepdims=True)
    h_scr[...] = (x * lax.rsqrt(ms + EPS) * gain_ref[...]).astype(BF16)

    def gate_piece():
        gt_dst[...] = _dot(h_scr[...], wg_ref[...])

    issued = set()

    def main_piece(nb):
        cols = slice(nb * PROJ_COLS, (nb + 1) * PROJ_COLS)
        w_ref, first = (wa_ref, 0) if nb < OFF_GQKV // PROJ_COLS else (wb_ref, OFF_GQKV)
        w_cols = slice(nb * PROJ_COLS - first, (nb + 1) * PROJ_COLS - first)
        pm_dst[:, cols] = _dot(h_scr[...], w_ref[:, w_cols]).astype(BF16)
        issued.add(nb)

    return gate_piece, {nb: functools.partial(main_piece, nb) for nb in range(N_MAIN // PROJ_COLS)}, issued


def _output_pieces(mix_scr, x_ref, wo_ref, fn_ref, y_scr, out_ref):
    def block_piece(nb):
        cols = slice(nb * OUT_COLS, (nb + 1) * OUT_COLS)
        y_scr[:, cols] = x_ref[:, cols] + _dot(mix_scr[...], wo_ref[:, cols])

    def norm_piece():
        y = y_scr[...]
        ms = jnp.mean(y * y, axis=-1, keepdims=True)
        out_ref[...] = y * lax.rsqrt(ms + EPS) * fn_ref[...]

    return [functools.partial(block_piece, nb) for nb in range(D_MODEL // OUT_COLS)] + [norm_piece]


def _front_end_pieces(pm_src, conv_ref, xbuf_scr, dsts):
    def load_piece():
        xbuf_scr[8:8 + TILE, :] = pm_src[:, OFF_GQKV:OFF_GQKV + 3 * G_WIDTH].astype(F32)

    def block_piece(part, cb):
        ones_bd = jnp.where(
            lax.broadcasted_iota(jnp.int32, (CONV_COLS, CONV_COLS), 0) // LANES
            == lax.broadcasted_iota(jnp.int32, (CONV_COLS, CONV_COLS), 1) // LANES, 1.0, 0.0).astype(BF16)
        cols = slice(part * G_WIDTH + cb * CONV_COLS, part * G_WIDTH + (cb + 1) * CONV_COLS)
        acc = conv_ref[CONV_WIDTH - 1:CONV_WIDTH, cols] * xbuf_scr[8:8 + TILE, cols]
        for j in range(CONV_WIDTH - 1):
            off = 8 - (CONV_WIDTH - 1) + j
            acc = acc + conv_ref[j:j + 1, cols] * xbuf_scr[off:off + TILE, cols]
        act = _silu(acc)
        if part < 2:
            act = act * lax.rsqrt(_dot((act * act).astype(BF16), ones_bd) + EPS)
        dsts[part][:, cb * CONV_COLS:(cb + 1) * CONV_COLS] = act

    def tail_piece():
        xbuf_scr[0:8, :] = xbuf_scr[TILE:TILE + 8, :]

    blocks = [functools.partial(block_piece, part, cb)
              for part in range(3) for cb in range(G_WIDTH // CONV_COLS)]
    return [load_piece] + blocks + [tail_piece]


def _mix(pm_ref, gt_ref, pv_ref, mnorm_ref, gnorm_ref, mask_ref, mix_scr,
         c_scr, n_scr, m_scr, s_scr, gq_scr, gk_scr, gv_scr, bc_scr, side_work, recycle):
    def side(n=1):
        for _ in range(n):
            if side_work:
                side_work.pop(0)()

    side()
    lane = lax.broadcasted_iota(jnp.int32, (TILE, LANES), 1)
    row = lax.broadcasted_iota(jnp.int32, (TILE, LANES), 0)
    is_m = lane < M_HEADS
    a = gt_ref[:, 0:LANES] + pv_ref[0:1, :]
    b = gt_ref[:, LANES:2 * LANES] + pv_ref[1:2, :]
    log_f = -_softplus(-b)
    log_alpha = -jnp.exp(pv_ref[2:3, :]) * _softplus(b)
    ab = jnp.where(is_m, a, _sigmoid(a))
    cum = _scan_rows(jnp.where(is_m, log_f, log_alpha), row, jnp.add, 0.0)
    side()
    m_prev = m_scr[...]
    cmax = _scan_rows(ab - cum, row, jnp.maximum, -jnp.inf)
    inter = cum + m_prev
    m_t = jnp.maximum(inter, cum + cmax)
    last = cum[TILE - 1:TILE, :]
    a_s = last - cum + ab
    m_new = jnp.maximum(last + m_prev, jnp.max(a_s, axis=0, keepdims=True))
    decay_m = jnp.exp(last + m_prev - m_new)
    e_last = jnp.exp(last)
    m_scr[...] = jnp.where(is_m[0:1, :], m_new, 0.0)
    row_form = jnp.where(is_m, ab - cum + np.float32(np.log(M_QK_DIM ** -0.5)), -cum).T

    w_inter = jnp.exp(inter - m_t) * (M_QK_DIM ** -0.5)
    e_neg_mt = jnp.exp(-m_t)
    m_cols = (cum - m_t, w_inter, jnp.exp(a_s - m_new))
    g_cols = (cum, ab, jnp.exp(cum), jnp.exp(last - cum))
    for h in range(M_HEADS):
        for i, colv in enumerate(m_cols):
            bc_scr[h * M_BC + i] = jnp.broadcast_to(colv[:, h:h + 1], (TILE, LANES))
        if h % 2 == 1:
            side()
    for j in range(G_HEADS):
        for i, colv in enumerate(g_cols):
            bc_scr[M_HEADS * M_BC + j * G_BC + i] = jnp.broadcast_to(
                colv[:, M_HEADS + j:M_HEADS + j + 1], (TILE, LANES))
        if j % 2 == 1:
            side()

    ri = lax.broadcasted_iota(jnp.int32, (TILE, TILE), 0)
    ci = lax.broadcasted_iota(jnp.int32, (TILE, TILE), 1)
    incl = ri >= ci

    mh = range(M_HEADS)
    mbc = lambda h, i: bc_scr[h * M_BC + i]
    q_m = {h: pm_ref[:, OFF_MQ + h * M_QK_DIM:OFF_MQ + (h + 1) * M_QK_DIM] for h in mh}
    k_m = {h: pm_ref[:, OFF_MK + h * M_QK_DIM:OFF_MK + (h + 1) * M_QK_DIM] for h in mh}
    v_m = {h: pm_ref[:, OFF_MV + h * M_V_DIM:OFF_MV + (h + 1) * M_V_DIM] for h in mh}
    p_f = {}
    for h in mh:
        p_f[h] = (jnp.where(incl, jnp.exp(_wide(mbc(h, 0)) + row_form[h:h + 1, :]), 0.0)
                  * _dot_nt(q_m[h], k_m[h]))
        if h % 2 == 1:
            side()
    p_sum = {h: jnp.sum(p_f[h], axis=-1, keepdims=True) for h in mh}
    c_old = {h: c_scr[h] for h in mh}
    n_old = {h: n_scr[h, 0:1, :] for h in mh}
    q_n = {h: jnp.sum(q_m[h].astype(F32) * n_old[h], axis=-1, keepdims=True) for h in mh}
    side()
    num = {}
    for h in mh:
        num[h] = _wide(mbc(h, 1)) * _dot(q_m[h], c_old[h].astype(BF16)) + _dot(p_f[h].astype(BF16), v_m[h])
        if h % 2 == 1:
            side()
    for h in mh:
        kw = k_m[h].astype(F32) * mbc(h, 2)
        c_scr[h] = decay_m[0:1, h:h + 1] * c_old[h] + _dot_tn(kw.astype(BF16), v_m[h])
        n_scr[h, 0:1, :] = decay_m[0:1, h:h + 1] * n_old[h] + jnp.sum(kw, axis=0, keepdims=True)
    side()
    den = {h: jnp.maximum(jnp.abs(w_inter[:, h:h + 1] * q_n[h] + p_sum[h]), e_neg_mt[:, h:h + 1]) for h in mh}
    hm = {h: num[h] * (1.0 / den[h]) for h in mh}
    ms_m = {h: jnp.sum(hm[h] * hm[h], axis=-1, keepdims=True) * (1.0 / M_V_DIM) for h in mh}
    side()
    for h in mh:
        cols = slice(h * M_V_DIM, (h + 1) * M_V_DIM)
        o_gate = _sigmoid(pm_ref[:, OFF_MO + h * M_V_DIM:OFF_MO + (h + 1) * M_V_DIM].astype(F32))
        z = pm_ref[:, OFF_MZ + h * M_V_DIM:OFF_MZ + (h + 1) * M_V_DIM].astype(F32)
        y = hm[h] * lax.rsqrt(ms_m[h] + EPS) * mnorm_ref[0:1, cols] * o_gate * _silu(z)
        mix_scr[:, cols] = y.astype(BF16)
        if h % 2 == 1:
            side()

    g_scale = G_DIM ** -0.5
    heads = range(G_HEADS)
    pairs = range(G_HEADS // 2)
    hs = lambda j: slice(j * G_DIM, (j + 1) * G_DIM)
    half_of = lambda j: slice((j % 2) * G_DIM, (j % 2 + 1) * G_DIM)
    bc = lambda j, i: bc_scr[M_HEADS * M_BC + j * G_BC + i]
    top, bot = slice(0, HALF), slice(HALF, TILE)
    k_b = {j: gk_scr[:, hs(j)].astype(BF16) for j in heads}
    q_b = {j: (gq_scr[:, hs(j)] * g_scale).astype(BF16) for j in heads}
    side()
    gam, a_qk, low = {}, {}, {}
    for j in heads:
        gam[j] = jnp.where(incl, jnp.exp(_wide(bc(j, 0)) + row_form[M_HEADS + j:M_HEADS + j + 1, :]), 0.0)
        a_qk[j] = (_dot_nt(q_b[j], k_b[j]) * gam[j]).astype(BF16)
        low[j] = (_wide(bc(j, 1)) * _dot_nt(k_b[j], k_b[j]) * gam[j]).astype(BF16)
        if j % 2 == 1:
            side()
    low_f = {j: _fold(low[j]) for j in heads}
    dinv = {j: (mask_ref[N_LEVELS] - low_f[j] * mask_ref[0]).astype(F32) for j in heads}
    for lvl in range(1, N_LEVELS):
        s = 1 << lvl
        d_b = {j: dinv[j].astype(BF16) for j in heads}
        if s < ROW_GROUP:
            e_b = {j: _dot(low_f[j] * mask_ref[lvl], _unfold(d_b[j])).astype(BF16) for j in heads}
            side()
            dinv = {j: dinv[j] - _dot(d_b[j], _unfold(e_b[j])) for j in heads}
        else:
            e_b = {j: _dot(_odd_blocks(low_f[j] * mask_ref[lvl], s), _unfold(d_b[j])).astype(BF16) for j in heads}
            side()
            f_odd = {j: _dot(_odd_blocks(d_b[j], s), _unfold(_spread_odd_blocks(e_b[j], s))) for j in heads}
            dinv = {j: dinv[j] - _spread_odd_blocks(f_odd[j], s) for j in heads}
        side()
    recycle(OFF_MQ, OFF_GQKV)
    rhs = {j: jnp.concatenate([gv_scr[:, hs(j)] * bc(j, 1), gk_scr[:, hs(j)] * (bc(j, 1) * bc(j, 2))], axis=1)
           for j in heads}
    sol_top = {j: _dot(dinv[j][:, :HALF].astype(BF16), rhs[j][top].astype(BF16)) for j in heads}
    side()
    res_bot = {j: (rhs[j][bot] - _dot(low[j][bot, top], sol_top[j].astype(BF16))).astype(BF16) for j in heads}
    sol_bot = {j: _dot(dinv[j][:, HALF:].astype(BF16), res_bot[j]) for j in heads}
    side()
    sol = {j: jnp.concatenate([sol_top[j], sol_bot[j]], axis=0) for j in heads}
    s_old = {p: s_scr[p] for p in pairs}
    s_bd = {p: _unfold(s_old[p].astype(BF16)) for p in pairs}
    w_s2 = {p: _dot(jnp.concatenate([sol[2 * p][:, G_DIM:], sol[2 * p + 1][:, G_DIM:]], axis=1).astype(BF16),
                    s_bd[p]) for p in pairs}
    v_new = {j: (sol[j][:, :G_DIM] - w_s2[j // 2][:, half_of(j)]).astype(BF16) for j in heads}
    side()
    for p in pairs:
        upd = []
        for j in (2 * p, 2 * p + 1):
            kd = (gk_scr[:, hs(j)] * bc(j, 3)).astype(BF16)
            upd.append(e_last[0:1, M_HEADS + j:M_HEADS + j + 1] * s_old[p][:, half_of(j)] + _dot_tn(kd, v_new[j]))
        s_scr[p] = jnp.concatenate(upd, axis=1)
    side()
    q_s2 = {p: _dot(jnp.concatenate([gq_scr[:, hs(2 * p)] * (bc(2 * p, 2) * g_scale),
                                     gq_scr[:, hs(2 * p + 1)] * (bc(2 * p + 1, 2) * g_scale)],
                                    axis=1).astype(BF16), s_bd[p]) for p in pairs}
    og = {j: q_s2[j // 2][:, half_of(j)] + _dot(a_qk[j], v_new[j]) for j in heads}
    side()
    ms_g = {j: jnp.sum(og[j] * og[j], axis=-1, keepdims=True) * (1.0 / G_DIM) for j in heads}
    for j in heads:
        z = pm_ref[:, OFF_GZ + j * G_DIM:OFF_GZ + (j + 1) * G_DIM].astype(F32)
        y = og[j] * lax.rsqrt(ms_g[j] + EPS) * gnorm_ref[0:1, hs(j)] * _silu(z)
        mix_scr[:, M_WIDTH + j * G_DIM:M_WIDTH + (j + 1) * G_DIM] = y.astype(BF16)
        if j % 2 == 1:
            side()
    side(len(side_work))


def _layer_kernel(x_next_ref, x_cur_ref, again_ref, wa_ref, wb_ref, wg_ref, pv_ref, conv_ref, mnorm_ref, gnorm_ref,
                  mask_ref, wo_ref, fn_ref, out_ref,
                  pm_cur, gt_cur, pm_next, gt_next, h_scr, mix_scr, y_scr,
                  c_scr, n_scr, m_scr, s_scr, xbuf_scr,
                  gq_cur, gk_cur, gv_cur, gq_next, gk_next, gv_next, bc_scr, *, steps):
    g = pl.program_id(0)

    @pl.when(g == 0)
    def _():
        pm_cur[...] = jnp.zeros_like(pm_cur)
        gt_cur[...] = jnp.zeros_like(gt_cur)
        mix_scr[...] = jnp.zeros_like(mix_scr)
        gq_cur[...] = jnp.zeros_like(gq_cur)
        gk_cur[...] = jnp.zeros_like(gk_cur)
        gv_cur[...] = jnp.zeros_like(gv_cur)

    @pl.when(jnp.logical_or(g == 0, lax.rem(g + steps - 1, steps) == 0))
    def _():
        c_scr[...] = jnp.zeros_like(c_scr)
        n_scr[...] = jnp.zeros_like(n_scr)
        m_scr[...] = jnp.zeros_like(m_scr)
        s_scr[...] = jnp.zeros_like(s_scr)

    @pl.when(lax.rem(g, steps) == 0)
    def _():
        xbuf_scr[0:8, :] = jnp.zeros((8, 3 * G_WIDTH), F32)

    finish = _output_pieces(mix_scr, x_cur_ref, wo_ref, fn_ref, y_scr, out_ref)
    gate_piece, proj, projected = _projection_pieces(x_next_ref, again_ref, wa_ref, wb_ref, wg_ref, h_scr, pm_next, gt_next)
    front = _front_end_pieces(pm_next, conv_ref, xbuf_scr, (gq_next, gk_next, gv_next))
    qkv_blocks = [OFF_GQKV // PROJ_COLS + i for i in range(3 * G_WIDTH // PROJ_COLS)]
    rest = [proj[nb] for nb in sorted(proj) if nb not in qkv_blocks]
    load_piece, conv = front[0], front[1:]
    skip = lambda: None
    side_work = (finish + [gate_piece] + [proj[nb] for nb in qkv_blocks] + [load_piece]
                 + [rest[0], skip, rest[1], skip, skip, rest[2], skip, skip, skip]
                 + [rest[3], skip, rest[4], skip, skip]
                 + [p for c in conv[:6] for p in (c, skip)]
                 + conv[6:])
    assert len(rest) == 5

    def recycle(lo, hi):
        assert all(nb in projected for nb in range(lo // PROJ_COLS, hi // PROJ_COLS)), "projection piece not issued yet"
        pm_cur[:, lo:hi] = pm_next[:, lo:hi]

    _mix(pm_cur, gt_cur, pv_ref, mnorm_ref, gnorm_ref, mask_ref, mix_scr,
         c_scr, n_scr, m_scr, s_scr, gq_cur, gk_cur, gv_cur, bc_scr, side_work, recycle)

    recycle(OFF_GZ, N_MAIN)
    gt_cur[...] = gt_next[...]
    gq_cur[...] = gq_next[...]
    gk_cur[...] = gk_next[...]
    gv_cur[...] = gv_next[...]


def _layer(x2, again, w_a, w_b, w_gate, pvec, conv_w, mnorm, gnorm, w_out, fgain, steps):
    n_tiles = x2.shape[0] // TILE
    const = lambda g: (0, 0)
    single = dict(pipeline_mode=pl.Buffered(1))
    return pl.pallas_call(
        functools.partial(_layer_kernel, steps=steps),
        grid=(n_tiles + 2,),
        in_specs=[
            pl.BlockSpec((TILE, D_MODEL), lambda g: (jnp.minimum(g, n_tiles - 1), 0)),
            pl.BlockSpec((TILE, D_MODEL), lambda g: (jnp.maximum(g - 2, 0), 0)),
            pl.BlockSpec((1, D_MODEL), const),
            pl.BlockSpec((D_MODEL, OFF_GQKV), const, **single),
            pl.BlockSpec((D_MODEL, N_MAIN - OFF_GQKV), const, **single),
            pl.BlockSpec((D_MODEL, N_GATE), const, **single),
            pl.BlockSpec((8, LANES), const),
            pl.BlockSpec((CONV_WIDTH, 3 * G_WIDTH), const),
            pl.BlockSpec((1, M_WIDTH), const),
            pl.BlockSpec((1, G_WIDTH), const),
            pl.BlockSpec((N_LEVELS + 1, HALF, TILE), lambda g: (0, 0, 0), **single),
            pl.BlockSpec((MIX_WIDTH, D_MODEL), const, **single),
            pl.BlockSpec((1, D_MODEL), const),
        ],
        out_specs=pl.BlockSpec((TILE, D_MODEL), lambda g: (jnp.maximum(g - 2, 0), 0)),
        out_shape=jax.ShapeDtypeStruct(x2.shape, F32),
        scratch_shapes=[
            pltpu.VMEM((TILE, N_MAIN), BF16),
            pltpu.VMEM((TILE, N_GATE), F32),
            pltpu.VMEM((TILE, N_MAIN), BF16),
            pltpu.VMEM((TILE, N_GATE), F32),
            pltpu.VMEM((TILE, D_MODEL), BF16),
            pltpu.VMEM((TILE, MIX_WIDTH), BF16),
            pltpu.VMEM((TILE, D_MODEL), F32),
            pltpu.VMEM((M_HEADS, M_QK_DIM, M_V_DIM), F32),
            pltpu.VMEM((M_HEADS, 8, LANES), F32),
            pltpu.VMEM((1, LANES), F32),
            pltpu.VMEM((G_HEADS // 2, G_DIM, 2 * G_DIM), F32),
            pltpu.VMEM((TILE + 8, 3 * G_WIDTH), F32),
            pltpu.VMEM((TILE, G_WIDTH), F32),
            pltpu.VMEM((TILE, G_WIDTH), F32),
            pltpu.VMEM((TILE, G_WIDTH), F32),
            pltpu.VMEM((TILE, G_WIDTH), F32),
            pltpu.VMEM((TILE, G_WIDTH), F32),
            pltpu.VMEM((TILE, G_WIDTH), F32),
            pltpu.VMEM((M_HEADS * M_BC + G_HEADS * G_BC, TILE, LANES), F32),
        ],
        compiler_params=pltpu.CompilerParams(
            dimension_semantics=("arbitrary",), vmem_limit_bytes=VMEM_LIMIT),
    )(x2, x2, again, w_a, w_b, w_gate, pvec, conv_w, mnorm, gnorm, _level_masks(), w_out, fgain)


def _pad_lanes(v):
    return jnp.pad(v.astype(F32), (0, LANES - v.shape[0]))


def kernel(x, attn_norm, w_in, m_i_bias, m_f_bias, m_out_norm, g_conv, g_a_log, g_dt_bias,
           g_out_norm, w_out, final_norm):
    batch, seq, _ = x.shape
    assert attn_norm.shape[0] == 1, "single-layer block"
    assert seq % TILE == 0
    w = w_in[0].astype(BF16)
    splits = (M_QK_WIDTH, M_QK_WIDTH, M_WIDTH, M_WIDTH, M_WIDTH, M_HEADS, M_HEADS,
              3 * G_WIDTH, G_WIDTH, G_HEADS, G_HEADS)
    offs = [0]
    for s in splits:
        offs.append(offs[-1] + s)
    mi, mf, gb, ga = (w[:, offs[i]:offs[i + 1]] for i in (5, 6, 9, 10))
    w_a = w[:, offs[0]:offs[5]]
    w_b = w[:, offs[7]:offs[9]]
    assert w_a.shape[1] == OFF_GQKV and w_b.shape[1] == N_MAIN - OFF_GQKV
    zpad = jnp.zeros((D_MODEL, LANES - M_HEADS - G_HEADS), BF16)
    w_gate = jnp.concatenate([mi, gb, zpad, mf, ga, zpad], axis=1)
    zero4 = jnp.zeros((M_HEADS,), F32)
    pvec = jnp.stack([
        _pad_lanes(m_i_bias[0]),
        _pad_lanes(jnp.concatenate([m_f_bias[0].astype(F32), g_dt_bias[0].astype(F32)])),
        _pad_lanes(jnp.concatenate([zero4, g_a_log[0].astype(F32)])),
    ] + [jnp.zeros((LANES,), F32)] * 5)
    gnorm = jnp.tile(g_out_norm[0].astype(F32), G_HEADS)[None, :]

    x2 = x.reshape(batch * seq, D_MODEL)
    out = _layer(x2, attn_norm.astype(F32), w_a, w_b, w_gate, pvec, g_conv[0].astype(F32),
                 m_out_norm.astype(F32), gnorm, w_out[0].astype(BF16),
                 final_norm.astype(F32)[None, :], seq // TILE)
    return out.reshape(batch, seq, D_MODEL)
```

```python
import functools

import jax
import jax.numpy as jnp
import numpy as np
from jax import lax
from jax.experimental import pallas as pl
from jax.experimental.pallas import tpu as pltpu

F32 = jnp.float32
BF16 = jnp.bfloat16

D_MODEL = 1024
M_HEADS = 4
M_QK_DIM = 128
M_V_DIM = 256
M_QK_WIDTH = M_HEADS * M_QK_DIM
M_WIDTH = M_HEADS * M_V_DIM
G_HEADS = 8
G_DIM = 128
G_WIDTH = G_HEADS * G_DIM
CONV_WIDTH = 4
MIX_WIDTH = M_WIDTH + G_WIDTH
EPS = 1e-6
LANES = 128

OFF_MQ = 0
OFF_MK = OFF_MQ + M_QK_WIDTH
OFF_MV = OFF_MK + M_QK_WIDTH
OFF_MO = OFF_MV + M_WIDTH
OFF_MZ = OFF_MO + M_WIDTH
OFF_GQKV = OFF_MZ + M_WIDTH
OFF_GZ = OFF_GQKV + 3 * G_WIDTH
N_MAIN = OFF_GZ + G_WIDTH
N_GATE = 2 * LANES

PROJ_COLS = 1024
OUT_COLS = 512
CONV_COLS = 2 * LANES
TILE = 256
HALF = TILE // 2
N_LEVELS = 7
ROW_GROUP = 16
VMEM_LIMIT = 60 * 1024 * 1024

M_BC = 3
G_BC = 4


def _dot(a, b):
    return jnp.dot(a, b, preferred_element_type=F32)


def _dot_nt(a, b):
    return lax.dot_general(a, b, (((1,), (1,)), ((), ())), preferred_element_type=F32)


def _dot_tn(a, b):
    return lax.dot_general(a, b, (((0,), (0,)), ((), ())), preferred_element_type=F32)


def _scan_rows(x, row, combine, identity):
    shift = 1
    while shift < TILE:
        x = combine(x, jnp.where(row >= shift, pltpu.roll(x, shift, 0), identity))
        shift *= 2
    return x


def _softplus(x):
    return jnp.maximum(x, 0.0) + jnp.log1p(jnp.exp(-jnp.abs(x)))


def _sigmoid(x):
    return 0.5 + 0.5 * jnp.tanh(0.5 * x)


def _silu(x):
    half = 0.5 * x
    return half + half * jnp.tanh(half)


def _wide(x128):
    return jnp.concatenate([x128, x128], axis=1)


def _level_masks():
    i = np.arange(HALF)[:, None]
    j = np.arange(HALF)[None, :]
    out = []
    for l in range(N_LEVELS):
        s = 1 << l
        out.append((i // (2 * s) == j // (2 * s)) & ((i // s) % 2 == 1) & ((j // s) % 2 == 0))
    out.append(i == j)
    half = np.stack(out).astype(np.float32)
    return jnp.asarray(np.concatenate([half, half], axis=2), dtype=BF16)


def _fold(full):
    return jnp.concatenate([full[:HALF, :HALF], full[HALF:, HALF:]], axis=1)


def _unfold(folded):
    zero = jnp.zeros((HALF, HALF), folded.dtype)
    return jnp.concatenate([jnp.concatenate([folded[:, :HALF], zero], axis=1),
                            jnp.concatenate([zero, folded[:, HALF:]], axis=1)], axis=0)


def _odd_blocks(x, s):
    return jnp.concatenate([x[r:r + s] for r in range(s, HALF, 2 * s)], axis=0)


def _spread_odd_blocks(x, s):
    zero = jnp.zeros((s, x.shape[1]), x.dtype)
    parts = []
    for i in range(HALF // (2 * s)):
        parts += [zero, x[i * s:(i + 1) * s]]
    return jnp.concatenate(parts, axis=0)


def _projection_pieces(x_ref, gain_ref, wa_ref, wb_ref, wg_ref, h_scr, pm_dst, gt_dst):
    x = x_ref[...]
    ms = jnp.mean(x * x, axis=-1, keepdims=True)
    h_scr[...] = (x * lax.rsqrt(ms + EPS) * gain_ref[...]).astype(BF16)

    def gate_piece():
        gt_dst[...] = _dot(h_scr[...], wg_ref[...])

    issued = set()

    def main_piece(nb):
        cols = slice(nb * PROJ_COLS, (nb + 1) * PROJ_COLS)
        w_ref, first = (wa_ref, 0) if nb < OFF_GQKV // PROJ_COLS else (wb_ref, OFF_GQKV)
        w_cols = slice(nb * PROJ_COLS - first, (nb + 1) * PROJ_COLS - first)
        pm_dst[:, cols] = _dot(h_scr[...], w_ref[:, w_cols]).astype(BF16)
        issued.add(nb)

    return gate_piece, {nb: functools.partial(main_piece, nb) for nb in range(N_MAIN // PROJ_COLS)}, issued


def _output_pieces(mix_scr, x_ref, wo_ref, fn_ref, y_scr, out_ref):
    def block_piece(nb):
        cols = slice(nb * OUT_COLS, (nb + 1) * OUT_COLS)
        y_scr[:, cols] = x_ref[:, cols] + _dot(mix_scr[...], wo_ref[:, cols])

    def norm_piece():
        y = y_scr[...]
        ms = jnp.mean(y * y, axis=-1, keepdims=True)
        out_ref[...] = y * lax.rsqrt(ms + EPS) * fn_ref[...]

    return [functools.partial(block_piece, nb) for nb in range(D_MODEL // OUT_COLS)] + [norm_piece]


def _front_end_pieces(pm_src, conv_ref, xbuf_scr, dsts):
    def load_piece():
        xbuf_scr[8:8 + TILE, :] = pm_src[:, OFF_GQKV:OFF_GQKV + 3 * G_WIDTH].astype(F32)

    def block_piece(part, cb):
        ones_bd = jnp.where(
            lax.broadcasted_iota(jnp.int32, (CONV_COLS, CONV_COLS), 0) // LANES
            == lax.broadcasted_iota(jnp.int32, (CONV_COLS, CONV_COLS), 1) // LANES, 1.0, 0.0).astype(BF16)
        cols = slice(part * G_WIDTH + cb * CONV_COLS, part * G_WIDTH + (cb + 1) * CONV_COLS)
        acc = conv_ref[CONV_WIDTH - 1:CONV_WIDTH, cols] * xbuf_scr[8:8 + TILE, cols]
        for j in range(CONV_WIDTH - 1):
            off = 8 - (CONV_WIDTH - 1) + j
            acc = acc + conv_ref[j:j + 1, cols] * xbuf_scr[off:off + TILE, cols]
        act = _silu(acc)
        if part < 2:
            act = act * lax.rsqrt(_dot((act * act).astype(BF16), ones_bd) + EPS)
        dsts[part][:, cb * CONV_COLS:(cb + 1) * CONV_COLS] = act

    def tail_piece():
        xbuf_scr[0:8, :] = xbuf_scr[TILE:TILE + 8, :]

    blocks = [functools.partial(block_piece, part, cb)
              for part in range(3) for cb in range(G_WIDTH // CONV_COLS)]
    return [load_piece] + blocks + [tail_piece]


def _mix(pm_ref, gt_ref, pv_ref, mnorm_ref, gnorm_ref, mask_ref, mix_scr,
         c_scr, n_scr, m_scr, s_scr, gq_scr, gk_scr, gv_scr, bc_scr, side_work, recycle):
    def side(n=1):
        for _ in range(n):
            if side_work:
                side_work.pop(0)()

    side()
    lane = lax.broadcasted_iota(jnp.int32, (TILE, LANES), 1)
    row = lax.broadcasted_iota(jnp.int32, (TILE, LANES), 0)
    is_m = lane < M_HEADS
    a = gt_ref[:, 0:LANES] + pv_ref[0:1, :]
    b = gt_ref[:, LANES:2 * LANES] + pv_ref[1:2, :]
    log_f = -_softplus(-b)
    log_alpha = -jnp.exp(pv_ref[2:3, :]) * _softplus(b)
    ab = jnp.where(is_m, a, _sigmoid(a))
    cum = _scan_rows(jnp.where(is_m, log_f, log_alpha), row, jnp.add, 0.0)
    side()
    m_prev = m_scr[...]
    cmax = _scan_rows(ab - cum, row, jnp.maximum, -jnp.inf)
    inter = cum + m_prev
    m_t = jnp.maximum(inter, cum + cmax)
    last = cum[TILE - 1:TILE, :]
    a_s = last - cum + ab
    m_new = jnp.maximum(last + m_prev, jnp.max(a_s, axis=0, keepdims=True))
    decay_m = jnp.exp(last + m_prev - m_new)
    e_last = jnp.exp(last)
    m_scr[...] = jnp.where(is_m[0:1, :], m_new, 0.0)
    row_form = jnp.where(is_m, ab - cum + np.float32(np.log(M_QK_DIM ** -0.5)), -cum).T

    w_inter = jnp.exp(inter - m_t) * (M_QK_DIM ** -0.5)
    e_neg_mt = jnp.exp(-m_t)
    m_cols = (cum - m_t, w_inter, jnp.exp(a_s - m_new))
    g_cols = (cum, ab, jnp.exp(cum), jnp.exp(last - cum))
    for h in range(M_HEADS):
        for i, colv in enumerate(m_cols):
            bc_scr[h * M_BC + i] = jnp.broadcast_to(colv[:, h:h + 1], (TILE, LANES))
        if h % 2 == 1:
            side()
    for j in range(G_HEADS):
        for i, colv in enumerate(g_cols):
            bc_scr[M_HEADS * M_BC + j * G_BC + i] = jnp.broadcast_to(
                colv[:, M_HEADS + j:M_HEADS + j + 1], (TILE, LANES))
        if j % 2 == 1:
            side()

    ri = lax.broadcasted_iota(jnp.int32, (TILE, TILE), 0)
    ci = lax.broadcasted_iota(jnp.int32, (TILE, TILE), 1)
    incl = ri >= ci

    mh = range(M_HEADS)
    mbc = lambda h, i: bc_scr[h * M_BC + i]
    q_m = {h: pm_ref[:, OFF_MQ + h * M_QK_DIM:OFF_MQ + (h + 1) * M_QK_DIM] for h in mh}
    k_m = {h: pm_ref[:, OFF_MK + h * M_QK_DIM:OFF_MK + (h + 1) * M_QK_DIM] for h in mh}
    v_m = {h: pm_ref[:, OFF_MV + h * M_V_DIM:OFF_MV + (h + 1) * M_V_DIM] for h in mh}
    p_f = {}
    for h in mh:
        p_f[h] = (jnp.where(incl, jnp.exp(_wide(mbc(h, 0)) + row_form[h:h + 1, :]), 0.0)
                  * _dot_nt(q_m[h], k_m[h]))
        if h % 2 == 1:
            side()
    p_sum = {h: jnp.sum(p_f[h], axis=-1, keepdims=True) for h in mh}
    c_old = {h: c_scr[h] for h in mh}
    n_old = {h: n_scr[h, 0:1, :] for h in mh}
    q_n = {h: jnp.sum(q_m[h].astype(F32) * n_old[h], axis=-1, keepdims=True) for h in mh}
    side()
    num = {}
    for h in mh:
        num[h] = _wide(mbc(h, 1)) * _dot(q_m[h], c_old[h].astype(BF16)) + _dot(p_f[h].astype(BF16), v_m[h])
        if h % 2 == 1:
            side()
    for h in mh:
        kw = k_m[h].astype(F32) * mbc(h, 2)
        c_scr[h] = decay_m[0:1, h:h + 1] * c_old[h] + _dot_tn(kw.astype(BF16), v_m[h])
        n_scr[h, 0:1, :] = decay_m[0:1, h:h + 1] * n_old[h] + jnp.sum(kw, axis=0, keepdims=True)
    side()
    den = {h: jnp.maximum(jnp.abs(w_inter[:, h:h + 1] * q_n[h] + p_sum[h]), e_neg_mt[:, h:h + 1]) for h in mh}
    hm = {h: num[h] * (1.0 / den[h]) for h in mh}
    ms_m = {h: jnp.sum(hm[h] * hm[h], axis=-1, keepdims=True) * (1.0 / M_V_DIM) for h in mh}
    side()
    for h in mh:
        cols = slice(h * M_V_DIM, (h + 1) * M_V_DIM)
        o_gate = _sigmoid(pm_ref[:, OFF_MO + h * M_V_DIM:OFF_MO + (h + 1) * M_V_DIM].astype(F32))
        z = pm_ref[:, OFF_MZ + h * M_V_DIM:OFF_MZ + (h + 1) * M_V_DIM].astype(F32)
        y = hm[h] * lax.rsqrt(ms_m[h] + EPS) * mnorm_ref[0:1, cols] * o_gate * _silu(z)
        mix_scr[:, cols] = y.astype(BF16)
        if h % 2 == 1:
            side()

    g_scale = G_DIM ** -0.5
    heads = range(G_HEADS)
    pairs = range(G_HEADS // 2)
    hs = lambda j: slice(j * G_DIM, (j + 1) * G_DIM)
    half_of = lambda j: slice((j % 2) * G_DIM, (j % 2 + 1) * G_DIM)
    bc = lambda j, i: bc_scr[M_HEADS * M_BC + j * G_BC + i]
    top, bot = slice(0, HALF), slice(HALF, TILE)
    k_b = {j: gk_scr[:, hs(j)].astype(BF16) for j in heads}
    q_b = {j: (gq_scr[:, hs(j)] * g_scale).astype(BF16) for j in heads}
    side()
    gam, a_qk, low = {}, {}, {}
    for j in heads:
        gam[j] = jnp.where(incl, jnp.exp(_wide(bc(j, 0)) + row_form[M_HEADS + j:M_HEADS + j + 1, :]), 0.0)
        a_qk[j] = (_dot_nt(q_b[j], k_b[j]) * gam[j]).astype(BF16)
        low[j] = (_wide(bc(j, 1)) * _dot_nt(k_b[j], k_b[j]) * gam[j]).astype(BF16)
        if j % 2 == 1:
            side()
    low_f = {j: _fold(low[j]) for j in heads}
    dinv = {j: (mask_ref[N_LEVELS] - low_f[j] * mask_ref[0]).astype(F32) for j in heads}
    for lvl in range(1, N_LEVELS):
        s = 1 << lvl
        d_b = {j: dinv[j].astype(BF16) for j in heads}
        if s < ROW_GROUP:
            e_b = {j: _dot(low_f[j] * mask_ref[lvl], _unfold(d_b[j])).astype(BF16) for j in heads}
            side()
            dinv = {j: dinv[j] - _dot(d_b[j], _unfold(e_b[j])) for j in heads}
        else:
            e_b = {j: _dot(_odd_blocks(low_f[j] * mask_ref[lvl], s), _unfold(d_b[j])).astype(BF16) for j in heads}
            side()
            f_odd = {j: _dot(_odd_blocks(d_b[j], s), _unfold(_spread_odd_blocks(e_b[j], s))) for j in heads}
            dinv = {j: dinv[j] - _spread_odd_blocks(f_odd[j], s) for j in heads}
        side()
    recycle(OFF_MQ, OFF_GQKV)
    rhs = {j: jnp.concatenate([gv_scr[:, hs(j)] * bc(j, 1), gk_scr[:, hs(j)] * (bc(j, 1) * bc(j, 2))], axis=1)
           for j in heads}
    sol_top = {j: _dot(dinv[j][:, :HALF].astype(BF16), rhs[j][top].astype(BF16)) for j in heads}
    side()
    res_bot = {j: (rhs[j][bot] - _dot(low[j][bot, top], sol_top[j].astype(BF16))).astype(BF16) for j in heads}
    sol_bot = {j: _dot(dinv[j][:, HALF:].astype(BF16), res_bot[j]) for j in heads}
    side()
    sol = {j: jnp.concatenate([sol_top[j], sol_bot[j]], axis=0) for j in heads}
    s_old = {p: s_scr[p] for p in pairs}
    s_bd = {p: _unfold(s_old[p].astype(BF16)) for p in pairs}
    w_s2 = {p: _dot(jnp.concatenate([sol[2 * p][:, G_DIM:], sol[2 * p + 1][:, G_DIM:]], axis=1).astype(BF16),
                    s_bd[p]) for p in pairs}
    v_new = {j: (sol[j][:, :G_DIM] - w_s2[j // 2][:, half_of(j)]).astype(BF16) for j in heads}
    side()
    for p in pairs:
        upd = []
        for j in (2 * p, 2 * p + 1):
            kd = (gk_scr[:, hs(j)] * bc(j, 3)).astype(BF16)
            upd.append(e_last[0:1, M_HEADS + j:M_HEADS + j + 1] * s_old[p][:, half_of(j)] + _dot_tn(kd, v_new[j]))
        s_scr[p] = jnp.concatenate(upd, axis=1)
    side()
    q_s2 = {p: _dot(jnp.concatenate([gq_scr[:, hs(2 * p)] * (bc(2 * p, 2) * g_scale),
                                     gq_scr[:, hs(2 * p + 1)] * (bc(2 * p + 1, 2) * g_scale)],
                                    axis=1).astype(BF16), s_bd[p]) for p in pairs}
    og = {j: q_s2[j // 2][:, half_of(j)] + _dot(a_qk[j], v_new[j]) for j in heads}
    side()
    ms_g = {j: jnp.sum(og[j] * og[j], axis=-1, keepdims=True) * (1.0 / G_DIM) for j in heads}
    for j in heads:
        z = pm_ref[:, OFF_GZ + j * G_DIM:OFF_GZ + (j + 1) * G_DIM].astype(F32)
        y = og[j] * lax.rsqrt(ms_g[j] + EPS) * gnorm_ref[0:1, hs(j)] * _silu(z)
        mix_scr[:, M_WIDTH + j * G_DIM:M_WIDTH + (j + 1) * G_DIM] = y.astype(BF16)
        if j % 2 == 1:
            side()
    side(len(side_work))


def _layer_kernel(x_next_ref, x_cur_ref, again_ref, wa_ref, wb_ref, wg_ref, pv_ref, conv_ref, mnorm_ref, gnorm_ref,
                  mask_ref, wo_ref, fn_ref, out_ref,
                  pm_cur, gt_cur, pm_next, gt_next, h_scr, mix_scr, y_scr,
                  c_scr, n_scr, m_scr, s_scr, xbuf_scr,
                  gq_cur, gk_cur, gv_cur, gq_next, gk_next, gv_next, bc_scr, *, steps):
    g = pl.program_id(0)

    @pl.when(g == 0)
    def _():
        pm_cur[...] = jnp.zeros_like(pm_cur)
        gt_cur[...] = jnp.zeros_like(gt_cur)
        mix_scr[...] = jnp.zeros_like(mix_scr)
        gq_cur[...] = jnp.zeros_like(gq_cur)
        gk_cur[...] = jnp.zeros_like(gk_cur)
        gv_cur[...] = jnp.zeros_like(gv_cur)

    @pl.when(jnp.logical_or(g == 0, lax.rem(g + steps - 1, steps) == 0))
    def _():
        c_scr[...] = jnp.zeros_like(c_scr)
        n_scr[...] = jnp.zeros_like(n_scr)
        m_scr[...] = jnp.zeros_like(m_scr)
        s_scr[...] = jnp.zeros_like(s_scr)

    @pl.when(lax.rem(g, steps) == 0)
    def _():
        xbuf_scr[0:8, :] = jnp.zeros((8, 3 * G_WIDTH), F32)

    finish = _output_pieces(mix_scr, x_cur_ref, wo_ref, fn_ref, y_scr, out_ref)
    gate_piece, proj, projected = _projection_pieces(x_next_ref, again_ref, wa_ref, wb_ref, wg_ref, h_scr, pm_next, gt_next)
    front = _front_end_pieces(pm_next, conv_ref, xbuf_scr, (gq_next, gk_next, gv_next))
    qkv_blocks = [OFF_GQKV // PROJ_COLS + i for i in range(3 * G_WIDTH // PROJ_COLS)]
    rest = [proj[nb] for nb in sorted(proj) if nb not in qkv_blocks]
    load_piece, conv = front[0], front[1:]
    skip = lambda: None
    side_work = (finish + [gate_piece] + [proj[nb] for nb in qkv_blocks] + [load_piece]
                 + [rest[0], skip, rest[1], skip, skip, rest[2], skip, skip, skip]
                 + [rest[3], skip, rest[4], skip, skip]
                 + [p for c in conv[:6] for p in (c, skip)]
                 + conv[6:])
    assert len(rest) == 5

    def recycle(lo, hi):
        assert all(nb in projected for nb in range(lo // PROJ_COLS, hi // PROJ_COLS)), "projection piece not issued yet"
        pm_cur[:, lo:hi] = pm_next[:, lo:hi]

    _mix(pm_cur, gt_cur, pv_ref, mnorm_ref, gnorm_ref, mask_ref, mix_scr,
         c_scr, n_scr, m_scr, s_scr, gq_cur, gk_cur, gv_cur, bc_scr, side_work, recycle)

    recycle(OFF_GZ, N_MAIN)
    gt_cur[...] = gt_next[...]
    gq_cur[...] = gq_next[...]
    gk_cur[...] = gk_next[...]
    gv_cur[...] = gv_next[...]


def _layer(x2, again, w_a, w_b, w_gate, pvec, conv_w, mnorm, gnorm, w_out, fgain, steps):
    n_tiles = x2.shape[0] // TILE
    const = lambda g: (0, 0)
    single = dict(pipeline_mode=pl.Buffered(1))
    return pl.pallas_call(
        functools.partial(_layer_kernel, steps=steps),
        grid=(n_tiles + 2,),
        in_specs=[
            pl.BlockSpec((TILE, D_MODEL), lambda g: (jnp.minimum(g, n_tiles - 1), 0)),
            pl.BlockSpec((TILE, D_MODEL), lambda g: (jnp.maximum(g - 2, 0), 0)),
            pl.BlockSpec((1, D_MODEL), const),
            pl.BlockSpec((D_MODEL, OFF_GQKV), const, **single),
            pl.BlockSpec((D_MODEL, N_MAIN - OFF_GQKV), const, **single),
            pl.BlockSpec((D_MODEL, N_GATE), const, **single),
            pl.BlockSpec((8, LANES), const),
            pl.BlockSpec((CONV_WIDTH, 3 * G_WIDTH), const),
            pl.BlockSpec((1, M_WIDTH), const),
            pl.BlockSpec((1, G_WIDTH), const),
            pl.BlockSpec((N_LEVELS + 1, HALF, TILE), lambda g: (0, 0, 0), **single),
            pl.BlockSpec((MIX_WIDTH, D_MODEL), const, **single),
            pl.BlockSpec((1, D_MODEL), const),
        ],
        out_specs=pl.BlockSpec((TILE, D_MODEL), lambda g: (jnp.maximum(g - 2, 0), 0)),
        out_shape=jax.ShapeDtypeStruct(x2.shape, F32),
        scratch_shapes=[
            pltpu.VMEM((TILE, N_MAIN), BF16),
            pltpu.VMEM((TILE, N_GATE), F32),
            pltpu.VMEM((TILE, N_MAIN), BF16),
            pltpu.VMEM((TILE, N_GATE), F32),
            pltpu.VMEM((TILE, D_MODEL), BF16),
            pltpu.VMEM((TILE, MIX_WIDTH), BF16),
            pltpu.VMEM((TILE, D_MODEL), F32),
            pltpu.VMEM((M_HEADS, M_QK_DIM, M_V_DIM), F32),
            pltpu.VMEM((M_HEADS, 8, LANES), F32),
            pltpu.VMEM((1, LANES), F32),
            pltpu.VMEM((G_HEADS // 2, G_DIM, 2 * G_DIM), F32),
            pltpu.VMEM((TILE + 8, 3 * G_WIDTH), F32),
            pltpu.VMEM((TILE, G_WIDTH), F32),
            pltpu.VMEM((TILE, G_WIDTH), F32),
            pltpu.VMEM((TILE, G_WIDTH), F32),
            pltpu.VMEM((TILE, G_WIDTH), F32),
            pltpu.VMEM((TILE, G_WIDTH), F32),
            pltpu.VMEM((TILE, G_WIDTH), F32),
            pltpu.VMEM((M_HEADS * M_BC + G_HEADS * G_BC, TILE, LANES), F32),
        ],
        compiler_params=pltpu.CompilerParams(
            dimension_semantics=("arbitrary",), vmem_limit_bytes=VMEM_LIMIT),
    )(x2, x2, again, w_a, w_b, w_gate, pvec, conv_w, mnorm, gnorm, _level_masks(), w_out, fgain)


def _pad_lanes(v):
    return jnp.pad(v.astype(F32), (0, LANES - v.shape[0]))


def kernel(x, attn_norm, w_in, m_i_bias, m_f_bias, m_out_norm, g_conv, g_a_log, g_dt_bias,
           g_out_norm, w_out, final_norm):
    batch, seq, _ = x.shape
    assert attn_norm.shape[0] == 1, "single-layer block"
    assert seq % TILE == 0
    w = w_in[0]
    splits =(M_QK_WIDTH, M_QK_WIDTH, M_WIDTH, M_WIDTH, M_WIDTH, M_HEADS, M_HEADS,
              3 * G_WIDTH, G_WIDTH, G_HEADS, G_HEADS)
    offs = [0]
    for s in splits:
        offs.append(offs[-1] + s)
    mi, mf, gb, ga = (w[:, offs[i]:offs[i + 1]] for i in (5, 6, 9, 10))
    w_a = w[:, offs[0]:offs[5]].astype(BF16)
    w_b = w[:, offs[7]:offs[9]].astype(BF16)
    assert w_a.shape[1] == OFF_GQKV and w_b.shape[1] == N_MAIN - OFF_GQKV
    zpad = jnp.zeros((D_MODEL, LANES - M_HEADS - G_HEADS), w.dtype)
    w_gate = jnp.concatenate([mi, gb, zpad, mf, ga, zpad], axis=1).astype(BF16)
    zero4 = jnp.zeros((M_HEADS,), F32)
    pvec = jnp.stack([
        _pad_lanes(m_i_bias[0]),
        _pad_lanes(jnp.concatenate([m_f_bias[0].astype(F32), g_dt_bias[0].astype(F32)])),
        _pad_lanes(jnp.concatenate([zero4, g_a_log[0].astype(F32)])),
    ] + [jnp.zeros((LANES,), F32)] * 5)
    gnorm = jnp.tile(g_out_norm[0].astype(F32), G_HEADS)[None, :]

    x2 = x.reshape(batch * seq, D_MODEL)
    out = _layer(x2, attn_norm.astype(F32), w_a, w_b, w_gate, pvec, g_conv[0].astype(F32),
                 m_out_norm.astype(F32), gnorm, w_out[0].astype(BF16),
                 final_norm.astype(F32)[None, :], seq // TILE)
    return out.reshape(batch, seq, D_MODEL)
```

```python
import functools

import jax
import jax.numpy as jnp
import numpy as np
from jax import lax
from jax.experimental import pallas as pl
from jax.experimental.pallas import tpu as pltpu

F32 = jnp.float32
BF16 = jnp.bfloat16

D_MODEL = 1024
M_HEADS = 4
M_QK_DIM = 128
M_V_DIM = 256
M_QK_WIDTH = M_HEADS * M_QK_DIM
M_WIDTH = M_HEADS * M_V_DIM
G_HEADS = 8
G_DIM = 128
G_WIDTH = G_HEADS * G_DIM
CONV_WIDTH = 4
MIX_WIDTH = M_WIDTH + G_WIDTH
EPS = 1e-6
LANES = 128

OFF_MQ = 0
OFF_MK = OFF_MQ + M_QK_WIDTH
OFF_MV = OFF_MK + M_QK_WIDTH
OFF_MO = OFF_MV + M_WIDTH
OFF_MZ = OFF_MO + M_WIDTH
OFF_GQKV = OFF_MZ + M_WIDTH
OFF_GZ = OFF_GQKV + 3 * G_WIDTH
N_MAIN = OFF_GZ + G_WIDTH
N_GATE = 2 * LANES
N_QKV = OFF_MO
ACT_MO, ACT_MZ, ACT_GZ = 0, M_WIDTH, 2 * M_WIDTH
N_ACT = 2 * M_WIDTH + G_WIDTH

PROJ_COLS = 1024
OUT_COLS = 512
CONV_COLS = 2 * LANES
TILE = 256
HALF = TILE // 2
N_LEVELS = 7
ROW_GROUP = 16
VMEM_LIMIT = 60 * 1024 * 1024

M_BC = 3
G_BC = 4


def _dot(a, b):
    return jnp.dot(a, b, preferred_element_type=F32)


def _dot_nt(a, b):
    return lax.dot_general(a, b, (((1,), (1,)), ((), ())), preferred_element_type=F32)


def _dot_tn(a, b):
    return lax.dot_general(a, b, (((0,), (0,)), ((), ())), preferred_element_type=F32)


def _scan_rows(x, row, combine, identity):
    shift = 1
    while shift < TILE:
        x = combine(x, jnp.where(row >= shift, pltpu.roll(x, shift, 0), identity))
        shift *= 2
    return x


def _softplus(x):
    return jnp.maximum(x, 0.0) + jnp.log1p(jnp.exp(-jnp.abs(x)))


def _sigmoid(x):
    return 0.5 + 0.5 * jnp.tanh(0.5 * x)


def _silu(x):
    half = 0.5 * x
    return half + half * jnp.tanh(half)


def _wide(x128):
    return jnp.concatenate([x128, x128], axis=1)


def _level_masks():
    i = np.arange(HALF)[:, None]
    j = np.arange(HALF)[None, :]
    out = []
    for l in range(N_LEVELS):
        s = 1 << l
        out.append((i // (2 * s) == j // (2 * s)) & ((i // s) % 2 == 1) & ((j // s) % 2 == 0))
    out.append(i == j)
    half = np.stack(out).astype(np.float32)
    return jnp.asarray(np.concatenate([half, half], axis=2), dtype=BF16)


def _fold(full):
    return jnp.concatenate([full[:HALF, :HALF], full[HALF:, HALF:]], axis=1)


def _unfold(folded):
    zero = jnp.zeros((HALF, HALF), folded.dtype)
    return jnp.concatenate([jnp.concatenate([folded[:, :HALF], zero], axis=1),
                            jnp.concatenate([zero, folded[:, HALF:]], axis=1)], axis=0)


def _odd_blocks(x, s):
    return jnp.concatenate([x[r:r + s] for r in range(s, HALF, 2 * s)], axis=0)


def _spread_odd_blocks(x, s):
    zero = jnp.zeros((s, x.shape[1]), x.dtype)
    parts = []
    for i in range(HALF // (2 * s)):
        parts += [zero, x[i * s:(i + 1) * s]]
    return jnp.concatenate(parts, axis=0)


def _projection_pieces(x_ref, gain_ref, wa_ref, wb_ref, wg_ref, h_scr, qkv_dst, act_dst, xbuf_scr, gt_dst):
    x = x_ref[...]
    ms = jnp.mean(x * x, axis=-1, keepdims=True)
    h_scr[...] = (x * lax.rsqrt(ms + EPS) * gain_ref[...]).astype(BF16)

    def gate_piece():
        gt_dst[...] = _dot(h_scr[...], wg_ref[...])

    issued = set()

    def main_piece(nb):
        cols = slice(nb * PROJ_COLS, (nb + 1) * PROJ_COLS)
        w_ref, first = (wa_ref, 0) if nb < OFF_GQKV // PROJ_COLS else (wb_ref, OFF_GQKV)
        w_cols = slice(nb * PROJ_COLS - first, (nb + 1) * PROJ_COLS - first)
        res = _dot(h_scr[...], w_ref[:, w_cols])
        lo = nb * PROJ_COLS
        if lo < N_QKV:
            qkv_dst[:, cols] = res.astype(BF16)
        elif OFF_GQKV <= lo < OFF_GZ:
            xbuf_scr[8:8 + TILE, lo - OFF_GQKV:lo - OFF_GQKV + PROJ_COLS] = res
        else:
            dst = {OFF_MO: ACT_MO, OFF_MZ: ACT_MZ, OFF_GZ: ACT_GZ}[lo]
            act_dst[:, dst:dst + PROJ_COLS] = res
        issued.add(nb)

    return gate_piece, {nb: functools.partial(main_piece, nb) for nb in range(N_MAIN // PROJ_COLS)}, issued


def _output_pieces(mix_scr, x_ref, wo_ref, fn_ref, y_scr, out_ref):
    def block_piece(nb):
        cols = slice(nb * OUT_COLS, (nb + 1) * OUT_COLS)
        y_scr[:, cols] = x_ref[:, cols] + _dot(mix_scr[...], wo_ref[:, cols])

    def norm_piece():
        y = y_scr[...]
        ms = jnp.mean(y * y, axis=-1, keepdims=True)
        out_ref[...] = y * lax.rsqrt(ms + EPS) * fn_ref[...]

    return [functools.partial(block_piece, nb) for nb in range(D_MODEL // OUT_COLS)] + [norm_piece]


def _front_end_pieces(conv_ref, xbuf_scr, dsts):
    def block_piece(part, cb):
        ones_bd = jnp.where(
            lax.broadcasted_iota(jnp.int32, (CONV_COLS, CONV_COLS), 0) // LANES
            == lax.broadcasted_iota(jnp.int32, (CONV_COLS, CONV_COLS), 1) // LANES, 1.0, 0.0).astype(BF16)
        cols = slice(part * G_WIDTH + cb * CONV_COLS, part * G_WIDTH + (cb + 1) * CONV_COLS)
        acc = conv_ref[CONV_WIDTH - 1:CONV_WIDTH, cols] * xbuf_scr[8:8 + TILE, cols]
        for j in range(CONV_WIDTH - 1):
            off = 8 - (CONV_WIDTH - 1) + j
            acc = acc + conv_ref[j:j + 1, cols] * xbuf_scr[off:off + TILE, cols]
        act = _silu(acc)
        if part < 2:
            act = act * lax.rsqrt(_dot((act * act).astype(BF16), ones_bd) + EPS)
        dsts[part][:, cb * CONV_COLS:(cb + 1) * CONV_COLS] = act

    def tail_piece():
        xbuf_scr[0:8, :] = xbuf_scr[TILE:TILE + 8, :]

    blocks = [functools.partial(block_piece, part, cb)
              for part in range(3) for cb in range(G_WIDTH // CONV_COLS)]
    return blocks + [tail_piece]


def _mix(qkv_ref, act_ref, gt_ref, pv_ref, mnorm_ref, gnorm_ref, mask_ref, mix_scr,
         c_scr, n_scr, m_scr, s_scr, gq_scr, gk_scr, gv_scr, bc_scr, side_work, recycle):
    def side(n=1):
        for _ in range(n):
            if side_work:
                side_work.pop(0)()

    side()
    lane = lax.broadcasted_iota(jnp.int32, (TILE, LANES), 1)
    row = lax.broadcasted_iota(jnp.int32, (TILE, LANES), 0)
    is_m = lane < M_HEADS
    a = gt_ref[:, 0:LANES] + pv_ref[0:1, :]
    b = gt_ref[:, LANES:2 * LANES] + pv_ref[1:2, :]
    log_f = -_softplus(-b)
    log_alpha = -jnp.exp(pv_ref[2:3, :]) * _softplus(b)
    ab = jnp.where(is_m, a, _sigmoid(a))
    cum = _scan_rows(jnp.where(is_m, log_f, log_alpha), row, jnp.add, 0.0)
    side()
    m_prev = m_scr[...]
    cmax = _scan_rows(ab - cum, row, jnp.maximum, -jnp.inf)
    inter = cum + m_prev
    m_t = jnp.maximum(inter, cum + cmax)
    last = cum[TILE - 1:TILE, :]
    a_s = last - cum + ab
    m_new = jnp.maximum(last + m_prev, jnp.max(a_s, axis=0, keepdims=True))
    decay_m = jnp.exp(last + m_prev - m_new)
    e_last = jnp.exp(last)
    m_scr[...] = jnp.where(is_m[0:1, :], m_new, 0.0)
    row_form = jnp.where(is_m, ab - cum + np.float32(np.log(M_QK_DIM ** -0.5)), -cum).T

    w_inter = jnp.exp(inter - m_t) * (M_QK_DIM ** -0.5)
    e_neg_mt = jnp.exp(-m_t)
    m_cols = (cum - m_t, w_inter, jnp.exp(a_s - m_new))
    g_cols = (cum, ab, jnp.exp(cum), jnp.exp(last - cum))
    for h in range(M_HEADS):
        for i, colv in enumerate(m_cols):
            bc_scr[h * M_BC + i] = jnp.broadcast_to(colv[:, h:h + 1], (TILE, LANES))
        if h % 2 == 1:
            side()
    for j in range(G_HEADS):
        for i, colv in enumerate(g_cols):
            bc_scr[M_HEADS * M_BC + j * G_BC + i] = jnp.broadcast_to(
                colv[:, M_HEADS + j:M_HEADS + j + 1], (TILE, LANES))
        if j % 2 == 1:
            side()

    ri = lax.broadcasted_iota(jnp.int32, (TILE, TILE), 0)
    ci = lax.broadcasted_iota(jnp.int32, (TILE, TILE), 1)
    incl = ri >= ci

    mh = range(M_HEADS)
    mbc = lambda h, i: bc_scr[h * M_BC + i]
    q_m = {h: qkv_ref[:, OFF_MQ + h * M_QK_DIM:OFF_MQ + (h + 1) * M_QK_DIM] for h in mh}
    k_m = {h: qkv_ref[:, OFF_MK + h * M_QK_DIM:OFF_MK + (h + 1) * M_QK_DIM] for h in mh}
    v_m = {h: qkv_ref[:, OFF_MV + h * M_V_DIM:OFF_MV + (h + 1) * M_V_DIM] for h in mh}
    p_f = {}
    for h in mh:
        p_f[h] = (jnp.where(incl, jnp.exp(_wide(mbc(h, 0)) + row_form[h:h + 1, :]), 0.0)
                  * _dot_nt(q_m[h], k_m[h]))
        if h % 2 == 1:
            side()
    p_sum = {h: jnp.sum(p_f[h], axis=-1, keepdims=True) for h in mh}
    c_old = {h: c_scr[h] for h in mh}
    n_old = {h: n_scr[h, 0:1, :] for h in mh}
    q_n = {h: jnp.sum(q_m[h].astype(F32) * n_old[h], axis=-1, keepdims=True) for h in mh}
    side()
    num = {}
    for h in mh:
        num[h] = _wide(mbc(h, 1)) * _dot(q_m[h], c_old[h].astype(BF16)) + _dot(p_f[h].astype(BF16), v_m[h])
        if h % 2 == 1:
            side()
    for h in mh:
        kw = k_m[h].astype(F32) * mbc(h, 2)
        c_scr[h] = decay_m[0:1, h:h + 1] * c_old[h] + _dot_tn(kw.astype(BF16), v_m[h])
        n_scr[h, 0:1, :] = decay_m[0:1, h:h + 1] * n_old[h] + jnp.sum(kw, axis=0, keepdims=True)
    side()
    den = {h: jnp.maximum(jnp.abs(w_inter[:, h:h + 1] * q_n[h] + p_sum[h]), e_neg_mt[:, h:h + 1]) for h in mh}
    hm = {h: num[h] * (1.0 / den[h]) for h in mh}
    ms_m = {h: jnp.sum(hm[h] * hm[h], axis=-1, keepdims=True) * (1.0 / M_V_DIM) for h in mh}
    side()
    for h in mh:
        cols = slice(h * M_V_DIM, (h + 1) * M_V_DIM)
        o_gate = _sigmoid(act_ref[:, ACT_MO + h * M_V_DIM:ACT_MO + (h + 1) * M_V_DIM])
        z = act_ref[:, ACT_MZ + h * M_V_DIM:ACT_MZ + (h + 1) * M_V_DIM]
        y = hm[h] * lax.rsqrt(ms_m[h] + EPS) * mnorm_ref[0:1, cols] * o_gate * _silu(z)
        mix_scr[:, cols] = y.astype(BF16)
        if h % 2 == 1:
            side()

    g_scale = G_DIM ** -0.5
    heads = range(G_HEADS)
    pairs = range(G_HEADS // 2)
    hs = lambda j: slice(j * G_DIM, (j + 1) * G_DIM)
    half_of = lambda j: slice((j % 2) * G_DIM, (j % 2 + 1) * G_DIM)
    bc = lambda j, i: bc_scr[M_HEADS * M_BC + j * G_BC + i]
    top, bot = slice(0, HALF), slice(HALF, TILE)
    k_b = {j: gk_scr[:, hs(j)].astype(BF16) for j in heads}
    q_b = {j: (gq_scr[:, hs(j)] * g_scale).astype(BF16) for j in heads}
    side()
    gam, a_qk, low = {}, {}, {}
    for j in heads:
        gam[j] = jnp.where(incl, jnp.exp(_wide(bc(j, 0)) + row_form[M_HEADS + j:M_HEADS + j + 1, :]), 0.0)
        a_qk[j] = (_dot_nt(q_b[j], k_b[j]) * gam[j]).astype(BF16)
        low[j] = (_wide(bc(j, 1)) * _dot_nt(k_b[j], k_b[j]) * gam[j]).astype(BF16)
        if j % 2 == 1:
            side()
    low_f = {j: _fold(low[j]) for j in heads}
    dinv = {j: (mask_ref[N_LEVELS] - low_f[j] * mask_ref[0]).astype(F32) for j in heads}
    for lvl in range(1, N_LEVELS):
        s = 1 << lvl
        d_b = {j: dinv[j].astype(BF16) for j in heads}
        if s < ROW_GROUP:
            e_b = {j: _dot(low_f[j] * mask_ref[lvl], _unfold(d_b[j])).astype(BF16) for j in heads}
            side()
            dinv = {j: dinv[j] - _dot(d_b[j], _unfold(e_b[j])) for j in heads}
        else:
            e_b = {j: _dot(_odd_blocks(low_f[j] * mask_ref[lvl], s), _unfold(d_b[j])).astype(BF16) for j in heads}
            side()
            f_odd = {j: _dot(_odd_blocks(d_b[j], s), _unfold(_spread_odd_blocks(e_b[j], s))) for j in heads}
            dinv = {j: dinv[j] - _spread_odd_blocks(f_odd[j], s) for j in heads}
        side()
    recycle()
    rhs = {j: jnp.concatenate([gv_scr[:, hs(j)] * bc(j, 1), gk_scr[:, hs(j)] * (bc(j, 1) * bc(j, 2))], axis=1)
           for j in heads}
    sol_top = {j: _dot(dinv[j][:, :HALF].astype(BF16), rhs[j][top].astype(BF16)) for j in heads}
    side()
    res_bot = {j: (rhs[j][bot] - _dot(low[j][bot, top], sol_top[j].astype(BF16))).astype(BF16) for j in heads}
    sol_bot = {j: _dot(dinv[j][:, HALF:].astype(BF16), res_bot[j]) for j in heads}
    side()
    sol = {j: jnp.concatenate([sol_top[j], sol_bot[j]], axis=0) for j in heads}
    s_old = {p: s_scr[p] for p in pairs}
    s_bd = {p: _unfold(s_old[p].astype(BF16)) for p in pairs}
    w_s2 = {p: _dot(jnp.concatenate([sol[2 * p][:, G_DIM:], sol[2 * p + 1][:, G_DIM:]], axis=1).astype(BF16),
                    s_bd[p]) for p in pairs}
    v_new = {j: (sol[j][:, :G_DIM] - w_s2[j // 2][:, half_of(j)]).astype(BF16) for j in heads}
    side()
    for p in pairs:
        upd = []
        for j in (2 * p, 2 * p + 1):
            kd = (gk_scr[:, hs(j)] * bc(j, 3)).astype(BF16)
            upd.append(e_last[0:1, M_HEADS + j:M_HEADS + j + 1] * s_old[p][:, half_of(j)] + _dot_tn(kd, v_new[j]))
        s_scr[p] = jnp.concatenate(upd, axis=1)
    side()
    q_s2 = {p: _dot(jnp.concatenate([gq_scr[:, hs(2 * p)] * (bc(2 * p, 2) * g_scale),
                                     gq_scr[:, hs(2 * p + 1)] * (bc(2 * p + 1, 2) * g_scale)],
                                    axis=1).astype(BF16), s_bd[p]) for p in pairs}
    og = {j: q_s2[j // 2][:, half_of(j)] + _dot(a_qk[j], v_new[j]) for j in heads}
    side()
    ms_g = {j: jnp.sum(og[j] * og[j], axis=-1, keepdims=True) * (1.0 / G_DIM) for j in heads}
    for j in heads:
        z = act_ref[:, ACT_GZ + j * G_DIM:ACT_GZ + (j + 1) * G_DIM]
        y = og[j] * lax.rsqrt(ms_g[j] + EPS) * gnorm_ref[0:1, hs(j)] * _silu(z)
        mix_scr[:, M_WIDTH + j * G_DIM:M_WIDTH + (j + 1) * G_DIM] = y.astype(BF16)
        if j % 2 == 1:
            side()
    side(len(side_work))


def _layer_kernel(x_next_ref, x_cur_ref, again_ref, wa_ref, wb_ref, wg_ref, pv_ref, conv_ref, mnorm_ref, gnorm_ref,
                  mask_ref, wo_ref, fn_ref, out_ref,
                  qkv_cur, act_cur, gt_cur, qkv_next, act_next, gt_next, h_scr, mix_scr, y_scr,
                  c_scr, n_scr, m_scr, s_scr, xbuf_scr,
                  gq_cur, gk_cur, gv_cur, gq_next, gk_next, gv_next, bc_scr, *, steps):
    g = pl.program_id(0)

    @pl.when(g == 0)
    def _():
        qkv_cur[...] = jnp.zeros_like(qkv_cur)
        act_cur[...] = jnp.zeros_like(act_cur)
        gt_cur[...] = jnp.zeros_like(gt_cur)
        mix_scr[...] = jnp.zeros_like(mix_scr)
        gq_cur[...] = jnp.zeros_like(gq_cur)
        gk_cur[...] = jnp.zeros_like(gk_cur)
        gv_cur[...] = jnp.zeros_like(gv_cur)

    @pl.when(jnp.logical_or(g == 0, lax.rem(g + steps - 1, steps) == 0))
    def _():
        c_scr[...] = jnp.zeros_like(c_scr)
        n_scr[...] = jnp.zeros_like(n_scr)
        m_scr[...] = jnp.zeros_like(m_scr)
        s_scr[...] = jnp.zeros_like(s_scr)

    @pl.when(lax.rem(g, steps) == 0)
    def _():
        xbuf_scr[0:8, :] = jnp.zeros((8, 3 * G_WIDTH), F32)

    finish = _output_pieces(mix_scr, x_cur_ref, wo_ref, fn_ref, y_scr, out_ref)
    gate_piece, proj, projected = _projection_pieces(x_next_ref, again_ref, wa_ref, wb_ref, wg_ref, h_scr,
                                                     qkv_next, act_next, xbuf_scr, gt_next)
    conv = _front_end_pieces(conv_ref, xbuf_scr, (gq_next, gk_next, gv_next))
    qkv_blocks = [OFF_GQKV // PROJ_COLS + i for i in range(3 * G_WIDTH // PROJ_COLS)]
    rest = [proj[nb] for nb in sorted(proj) if nb not in qkv_blocks]
    skip = lambda: None
    side_work = (finish + [gate_piece] + [proj[nb] for nb in qkv_blocks] + [skip]
                 + [rest[0], skip, rest[1], skip, skip, rest[2], skip, skip, skip]
                 + [rest[3], skip, rest[4], skip, skip]
                 + [p for c in conv[:6] for p in (c, skip)]
                 + conv[6:])
    assert len(rest) == 5

    def recycle():
        assert all(nb in projected for nb in range(N_QKV // PROJ_COLS)), "projection piece not issued yet"
        qkv_cur[...] = qkv_next[...]

    _mix(qkv_cur, act_cur, gt_cur, pv_ref, mnorm_ref, gnorm_ref, mask_ref, mix_scr,
         c_scr, n_scr, m_scr, s_scr, gq_cur, gk_cur, gv_cur, bc_scr, side_work, recycle)

    assert len(projected) == N_MAIN // PROJ_COLS
    act_cur[...] = act_next[...]
    gt_cur[...] = gt_next[...]
    gq_cur[...] = gq_next[...]
    gk_cur[...] = gk_next[...]
    gv_cur[...] = gv_next[...]


def _layer(x2, again, w_a, w_b, w_gate, pvec, conv_w, mnorm, gnorm, w_out, fgain, steps):
    n_tiles = x2.shape[0] // TILE
    const = lambda g: (0, 0)
    single = dict(pipeline_mode=pl.Buffered(1))
    return pl.pallas_call(
        functools.partial(_layer_kernel, steps=steps),
        grid=(n_tiles + 2,),
        in_specs=[
            pl.BlockSpec((TILE, D_MODEL), lambda g: (jnp.minimum(g, n_tiles - 1), 0)),
            pl.BlockSpec((TILE, D_MODEL), lambda g: (jnp.maximum(g - 2, 0), 0)),
            pl.BlockSpec((1, D_MODEL), const),
            pl.BlockSpec((D_MODEL, OFF_GQKV), const, **single),
            pl.BlockSpec((D_MODEL, N_MAIN - OFF_GQKV), const, **single),
            pl.BlockSpec((D_MODEL, N_GATE), const, **single),
            pl.BlockSpec((8, LANES), const),
            pl.BlockSpec((CONV_WIDTH, 3 * G_WIDTH), const),
            pl.BlockSpec((1, M_WIDTH), const),
            pl.BlockSpec((1, G_WIDTH), const),
            pl.BlockSpec((N_LEVELS + 1, HALF, TILE), lambda g: (0, 0, 0), **single),
            pl.BlockSpec((MIX_WIDTH, D_MODEL), const, **single),
            pl.BlockSpec((1, D_MODEL), const),
        ],
        out_specs=pl.BlockSpec((TILE, D_MODEL), lambda g: (jnp.maximum(g - 2, 0), 0)),
        out_shape=jax.ShapeDtypeStruct(x2.shape, F32),
        scratch_shapes=[
            pltpu.VMEM((TILE, N_QKV), BF16),
            pltpu.VMEM((TILE, N_ACT), F32),
            pltpu.VMEM((TILE, N_GATE), F32),
            pltpu.VMEM((TILE, N_QKV), BF16),
            pltpu.VMEM((TILE, N_ACT), F32),
            pltpu.VMEM((TILE, N_GATE), F32),
            pltpu.VMEM((TILE, D_MODEL), BF16),
            pltpu.VMEM((TILE, MIX_WIDTH), BF16),
            pltpu.VMEM((TILE, D_MODEL), F32),
            pltpu.VMEM((M_HEADS, M_QK_DIM, M_V_DIM), F32),
            pltpu.VMEM((M_HEADS, 8, LANES), F32),
            pltpu.VMEM((1, LANES), F32),
            pltpu.VMEM((G_HEADS // 2, G_DIM, 2 * G_DIM), F32),
            pltpu.VMEM((TILE + 8, 3 * G_WIDTH), F32),
            pltpu.VMEM((TILE, G_WIDTH), F32),
            pltpu.VMEM((TILE, G_WIDTH), F32),
            pltpu.VMEM((TILE, G_WIDTH), F32),
            pltpu.VMEM((TILE, G_WIDTH), F32),
            pltpu.VMEM((TILE, G_WIDTH), F32),
            pltpu.VMEM((TILE, G_WIDTH), F32),
            pltpu.VMEM((M_HEADS * M_BC + G_HEADS * G_BC, TILE, LANES), F32),
        ],
        compiler_params=pltpu.CompilerParams(
            dimension_semantics=("arbitrary",), vmem_limit_bytes=VMEM_LIMIT),
    )(x2, x2, again, w_a, w_b, w_gate, pvec, conv_w, mnorm, gnorm, _level_masks(), w_out, fgain)


def _pad_lanes(v):
    return jnp.pad(v.astype(F32), (0, LANES - v.shape[0]))


def kernel(x, attn_norm, w_in, m_i_bias, m_f_bias, m_out_norm, g_conv, g_a_log, g_dt_bias,
           g_out_norm, w_out, final_norm):
    batch, seq, _ = x.shape
    assert attn_norm.shape[0] == 1, "single-layer block"
    assert seq % TILE == 0
    w = w_in[0]
    splits =(M_QK_WIDTH, M_QK_WIDTH, M_WIDTH, M_WIDTH, M_WIDTH, M_HEADS, M_HEADS,
              3 * G_WIDTH, G_WIDTH, G_HEADS, G_HEADS)
    offs = [0]
    for s in splits:
        offs.append(offs[-1] + s)
    mi, mf, gb, ga = (w[:, offs[i]:offs[i + 1]] for i in (5, 6, 9, 10))
    w_a = w[:, offs[0]:offs[5]].astype(BF16)
    w_b = w[:, offs[7]:offs[9]].astype(BF16)
    assert w_a.shape[1] == OFF_GQKV and w_b.shape[1] == N_MAIN - OFF_GQKV
    zpad = jnp.zeros((D_MODEL, LANES - M_HEADS - G_HEADS), w.dtype)
    w_gate = jnp.concatenate([mi, gb, zpad, mf, ga, zpad], axis=1).astype(BF16)
    zero4 = jnp.zeros((M_HEADS,), F32)
    pvec = jnp.stack([
        _pad_lanes(m_i_bias[0]),
        _pad_lanes(jnp.concatenate([m_f_bias[0].astype(F32), g_dt_bias[0].astype(F32)])),
        _pad_lanes(jnp.concatenate([zero4, g_a_log[0].astype(F32)])),
    ] + [jnp.zeros((LANES,), F32)] * 5)
    gnorm = jnp.tile(g_out_norm[0].astype(F32), G_HEADS)[None, :]

    x2 = x.reshape(batch * seq, D_MODEL)
    out = _layer(x2, attn_norm.astype(F32), w_a, w_b, w_gate, pvec, g_conv[0].astype(F32),
                 m_out_norm.astype(F32), gnorm, w_out[0].astype(BF16),
                 final_norm.astype(F32)[None, :], seq // TILE)
    return out.reshape(batch, seq, D_MODEL)
```

```python
import functools

import jax
import jax.numpy as jnp
import numpy as np
from jax import lax
from jax.experimental import pallas as pl
from jax.experimental.pallas import tpu as pltpu

F32 = jnp.float32
BF16 = jnp.bfloat16

D_MODEL = 1024
M_HEADS = 4
M_QK_DIM = 128
M_V_DIM = 256
M_QK_WIDTH = M_HEADS * M_QK_DIM
M_WIDTH = M_HEADS * M_V_DIM
G_HEADS = 8
G_DIM = 128
G_WIDTH = G_HEADS * G_DIM
CONV_WIDTH = 4
MIX_WIDTH = M_WIDTH + G_WIDTH
EPS = 1e-6
LANES = 128

OFF_MQ = 0
OFF_MK = OFF_MQ + M_QK_WIDTH
OFF_MV = OFF_MK + M_QK_WIDTH
OFF_MO = OFF_MV + M_WIDTH
OFF_MZ = OFF_MO + M_WIDTH
OFF_GQKV = OFF_MZ + M_WIDTH
OFF_GZ = OFF_GQKV + 3 * G_WIDTH
N_MAIN = OFF_GZ + G_WIDTH
N_GATE = 2 * LANES
N_QKV = OFF_MO
ACT_MO, ACT_MZ, ACT_GZ = 0, M_WIDTH, 2 * M_WIDTH
N_ACT = 2 * M_WIDTH + G_WIDTH

PROJ_COLS = 1024
OUT_COLS = 512
CONV_COLS = 2 * LANES
TILE = 256
HALF = TILE // 2
N_LEVELS = 7
ROW_GROUP = 16
VMEM_LIMIT = 60 * 1024 * 1024

M_BC = 3
G_BC = 4


def _dot(a, b):
    return jnp.dot(a, b, preferred_element_type=F32)


def _dot_nt(a, b):
    return lax.dot_general(a, b, (((1,), (1,)), ((), ())), preferred_element_type=F32)


def _dot_tn(a, b):
    return lax.dot_general(a, b, (((0,), (0,)), ((), ())), preferred_element_type=F32)


def _scan_rows(x, row, combine, identity):
    shift = 1
    while shift < TILE:
        x = combine(x, jnp.where(row >= shift, pltpu.roll(x, shift, 0), identity))
        shift *= 2
    return x


def _softplus(x):
    return jnp.maximum(x, 0.0) + jnp.log1p(jnp.exp(-jnp.abs(x)))


def _sigmoid(x):
    return 0.5 + 0.5 * jnp.tanh(0.5 * x)


def _silu(x):
    half = 0.5 * x
    return half + half * jnp.tanh(half)


def _wide(x128):
    return jnp.concatenate([x128, x128], axis=1)


def _level_masks():
    i = np.arange(HALF)[:, None]
    j = np.arange(HALF)[None, :]
    out = []
    for l in range(N_LEVELS):
        s = 1 << l
        out.append((i // (2 * s) == j // (2 * s)) & ((i // s) % 2 == 1) & ((j // s) % 2 == 0))
    out.append(i == j)
    half = np.stack(out).astype(np.float32)
    return jnp.asarray(np.concatenate([half, half], axis=2), dtype=BF16)


def _fold(full):
    return jnp.concatenate([full[:HALF, :HALF], full[HALF:, HALF:]], axis=1)


def _unfold(folded):
    zero = jnp.zeros((HALF, HALF), folded.dtype)
    return jnp.concatenate([jnp.concatenate([folded[:, :HALF], zero], axis=1),
                            jnp.concatenate([zero, folded[:, HALF:]], axis=1)], axis=0)


def _odd_blocks(x, s):
    return jnp.concatenate([x[r:r + s] for r in range(s, HALF, 2 * s)], axis=0)


def _spread_odd_blocks(x, s):
    zero = jnp.zeros((s, x.shape[1]), x.dtype)
    parts = []
    for i in range(HALF // (2 * s)):
        parts += [zero, x[i * s:(i + 1) * s]]
    return jnp.concatenate(parts, axis=0)


def _projection_pieces(x_ref, gain_ref, wa_ref, wb_ref, wg_ref, h_scr, qkv_dst, act_dst, xbuf_scr, gt_dst):
    x = x_ref[...]
    ms = jnp.mean(x * x, axis=-1, keepdims=True)
    h_scr[...] = (x * lax.rsqrt(ms + EPS) * gain_ref[...]).astype(BF16)

    def gate_piece():
        gt_dst[...] = _dot(h_scr[...], wg_ref[...])

    issued = set()

    def main_piece(nb):
        cols = slice(nb * PROJ_COLS, (nb + 1) * PROJ_COLS)
        w_ref, first = (wa_ref, 0) if nb < OFF_GQKV // PROJ_COLS else (wb_ref, OFF_GQKV)
        w_cols = slice(nb * PROJ_COLS - first, (nb + 1) * PROJ_COLS - first)
        res = _dot(h_scr[...], w_ref[:, w_cols])
        lo = nb * PROJ_COLS
        if lo < N_QKV:
            qkv_dst[:, cols] = res.astype(BF16)
        elif OFF_GQKV <= lo < OFF_GZ:
            xbuf_scr[8:8 + TILE, lo - OFF_GQKV:lo - OFF_GQKV + PROJ_COLS] = res
        else:
            dst = {OFF_MO: ACT_MO, OFF_MZ: ACT_MZ, OFF_GZ: ACT_GZ}[lo]
            act_dst[:, dst:dst + PROJ_COLS] = res
        issued.add(nb)

    return gate_piece, {nb: functools.partial(main_piece, nb) for nb in range(N_MAIN // PROJ_COLS)}, issued


def _output_pieces(mix_scr, x_ref, wo_ref, fn_ref, y_scr, out_ref):
    def block_piece(nb):
        cols = slice(nb * OUT_COLS, (nb + 1) * OUT_COLS)
        y_scr[:, cols] = x_ref[:, cols] + _dot(mix_scr[...], wo_ref[:, cols])

    def norm_piece():
        y = y_scr[...]
        ms = jnp.mean(y * y, axis=-1, keepdims=True)
        out_ref[...] = y * lax.rsqrt(ms + EPS) * fn_ref[...]

    return [functools.partial(block_piece, nb) for nb in range(D_MODEL // OUT_COLS)] + [norm_piece]


def _front_end_pieces(conv_ref, xbuf_scr, dsts):
    def block_piece(part, cb):
        ones_bd = jnp.where(
            lax.broadcasted_iota(jnp.int32, (CONV_COLS, CONV_COLS), 0) // LANES
            == lax.broadcasted_iota(jnp.int32, (CONV_COLS, CONV_COLS), 1) // LANES, 1.0, 0.0).astype(BF16)
        cols = slice(part * G_WIDTH + cb * CONV_COLS, part * G_WIDTH + (cb + 1) * CONV_COLS)
        acc = conv_ref[CONV_WIDTH - 1:CONV_WIDTH, cols] * xbuf_scr[8:8 + TILE, cols]
        for j in range(CONV_WIDTH - 1):
            off = 8 - (CONV_WIDTH - 1) + j
            acc = acc + conv_ref[j:j + 1, cols] * xbuf_scr[off:off + TILE, cols]
        act = _silu(acc)
        if part < 2:
            act = act * lax.rsqrt(_dot((act * act).astype(BF16), ones_bd) + EPS)
        dsts[part][:, cb * CONV_COLS:(cb + 1) * CONV_COLS] = act

    def tail_piece():
        xbuf_scr[0:8, :] = xbuf_scr[TILE:TILE + 8, :]

    blocks = [functools.partial(block_piece, part, cb)
              for part in range(3) for cb in range(G_WIDTH // CONV_COLS)]
    return blocks + [tail_piece]


def _mix(qkv_ref, act_ref, gt_ref, pv_ref, mnorm_ref, gnorm_ref, mask_ref, mix_scr,
         c_scr, n_scr, m_scr, s_scr, gq_scr, gk_scr, gv_scr, bc_scr, side_work, recycle):
    def side(n=1):
        for _ in range(n):
            if side_work:
                side_work.pop(0)()

    side()
    lane = lax.broadcasted_iota(jnp.int32, (TILE, LANES), 1)
    row = lax.broadcasted_iota(jnp.int32, (TILE, LANES), 0)
    is_m = lane < M_HEADS
    a = gt_ref[:, 0:LANES] + pv_ref[0:1, :]
    b = gt_ref[:, LANES:2 * LANES] + pv_ref[1:2, :]
    log_f = -_softplus(-b)
    log_alpha = -jnp.exp(pv_ref[2:3, :]) * _softplus(b)
    ab = jnp.where(is_m, a, _sigmoid(a))
    cum = _scan_rows(jnp.where(is_m, log_f, log_alpha), row, jnp.add, 0.0)
    side()
    m_prev = m_scr[...]
    cmax = _scan_rows(ab - cum, row, jnp.maximum, -jnp.inf)
    inter = cum + m_prev
    m_t = jnp.maximum(inter, cum + cmax)
    last = cum[TILE - 1:TILE, :]
    a_s = last - cum + ab
    m_new = jnp.maximum(last + m_prev, jnp.max(a_s, axis=0, keepdims=True))
    decay_m = jnp.exp(last + m_prev - m_new)
    e_last = jnp.exp(last)
    m_scr[...] = jnp.where(is_m[0:1, :], m_new, 0.0)
    row_form = jnp.where(is_m, ab - cum + np.float32(np.log(M_QK_DIM ** -0.5)), -cum).T

    w_inter = jnp.exp(inter - m_t) * (M_QK_DIM ** -0.5)
    e_neg_mt = jnp.exp(-m_t)
    m_cols = (cum - m_t, w_inter, jnp.exp(a_s - m_new))
    g_cols = (cum, ab, jnp.exp(cum), jnp.exp(last - cum))
    for h in range(M_HEADS):
        for i, colv in enumerate(m_cols):
            bc_scr[h * M_BC + i] = jnp.broadcast_to(colv[:, h:h + 1], (TILE, LANES))
        if h % 2 == 1:
            side()
    for j in range(G_HEADS):
        for i, colv in enumerate(g_cols):
            bc_scr[M_HEADS * M_BC + j * G_BC + i] = jnp.broadcast_to(
                colv[:, M_HEADS + j:M_HEADS + j + 1], (TILE, LANES))
        if j % 2 == 1:
            side()

    ri = lax.broadcasted_iota(jnp.int32, (TILE, TILE), 0)
    ci = lax.broadcasted_iota(jnp.int32, (TILE, TILE), 1)
    incl = ri >= ci

    mh = range(M_HEADS)
    mbc = lambda h, i: bc_scr[h * M_BC + i]
    q_m = {h: qkv_ref[:, OFF_MQ + h * M_QK_DIM:OFF_MQ + (h + 1) * M_QK_DIM] for h in mh}
    k_m = {h: qkv_ref[:, OFF_MK + h * M_QK_DIM:OFF_MK + (h + 1) * M_QK_DIM] for h in mh}
    v_m = {h: qkv_ref[:, OFF_MV + h * M_V_DIM:OFF_MV + (h + 1) * M_V_DIM] for h in mh}
    p_f = {}
    for h in mh:
        p_f[h] = (jnp.where(incl, jnp.exp(_wide(mbc(h, 0)) + row_form[h:h + 1, :]), 0.0)
                  * _dot_nt(q_m[h], k_m[h]))
        if h % 2 == 1:
            side()
    p_sum = {h: jnp.sum(p_f[h], axis=-1, keepdims=True) for h in mh}
    c_old = {h: c_scr[h] for h in mh}
    n_old = {h: n_scr[h, 0:1, :] for h in mh}
    q_n = {h: jnp.sum(q_m[h].astype(F32) * n_old[h], axis=-1, keepdims=True) for h in mh}
    side()
    num = {}
    for h in mh:
        num[h] = _wide(mbc(h, 1)) * _dot(q_m[h], c_old[h].astype(BF16)) + _dot(p_f[h].astype(BF16), v_m[h])
        if h % 2 == 1:
            side()
    for h in mh:
        kw = k_m[h].astype(F32) * mbc(h, 2)
        c_scr[h] = decay_m[0:1, h:h + 1] * c_old[h] + _dot_tn(kw.astype(BF16), v_m[h])
        n_scr[h, 0:1, :] = decay_m[0:1, h:h + 1] * n_old[h] + jnp.sum(kw, axis=0, keepdims=True)
    side()
    den = {h: jnp.maximum(jnp.abs(w_inter[:, h:h + 1] * q_n[h] + p_sum[h]), e_neg_mt[:, h:h + 1]) for h in mh}
    hm = {h: num[h] * (1.0 / den[h]) for h in mh}
    ms_m = {h: jnp.sum(hm[h] * hm[h], axis=-1, keepdims=True) * (1.0 / M_V_DIM) for h in mh}
    side()
    for h in mh:
        cols = slice(h * M_V_DIM, (h + 1) * M_V_DIM)
        o_gate = _sigmoid(act_ref[:, ACT_MO + h * M_V_DIM:ACT_MO + (h + 1) * M_V_DIM])
        z = act_ref[:, ACT_MZ + h * M_V_DIM:ACT_MZ + (h + 1) * M_V_DIM]
        y = hm[h] * lax.rsqrt(ms_m[h] + EPS) * mnorm_ref[0:1, cols] * o_gate * _silu(z)
        mix_scr[:, cols] = y.astype(BF16)
        if h % 2 == 1:
            side()

    g_scale = G_DIM ** -0.5
    heads = range(G_HEADS)
    pairs = range(G_HEADS // 2)
    hs = lambda j: slice(j * G_DIM, (j + 1) * G_DIM)
    half_of = lambda j: slice((j % 2) * G_DIM, (j % 2 + 1) * G_DIM)
    bc = lambda j, i: bc_scr[M_HEADS * M_BC + j * G_BC + i]
    top, bot = slice(0, HALF), slice(HALF, TILE)
    k_b = {j: gk_scr[:, hs(j)].astype(BF16) for j in heads}
    q_b = {j: (gq_scr[:, hs(j)] * g_scale).astype(BF16) for j in heads}
    side()
    gam, a_qk, low = {}, {}, {}
    for j in heads:
        gam[j] = jnp.where(incl, jnp.exp(_wide(bc(j, 0)) + row_form[M_HEADS + j:M_HEADS + j + 1, :]), 0.0)
        a_qk[j] = (_dot_nt(q_b[j], k_b[j]) * gam[j]).astype(BF16)
        low[j] = (_wide(bc(j, 1)) * _dot_nt(k_b[j], k_b[j]) * gam[j]).astype(BF16)
        if j % 2 == 1:
            side()
    low_f = {j: _fold(low[j]) for j in heads}
    dinv = {j: (mask_ref[N_LEVELS] - low_f[j] * mask_ref[0]).astype(F32) for j in heads}
    for lvl in range(1, N_LEVELS):
        s = 1 << lvl
        d_b = {j: dinv[j].astype(BF16) for j in heads}
        if s < ROW_GROUP:
            e_b = {j: _dot(low_f[j] * mask_ref[lvl], _unfold(d_b[j])).astype(BF16) for j in heads}
            side()
            dinv = {j: dinv[j] - _dot(d_b[j], _unfold(e_b[j])) for j in heads}
        else:
            e_b = {j: _dot(_odd_blocks(low_f[j] * mask_ref[lvl], s), _unfold(d_b[j])).astype(BF16) for j in heads}
            side()
            f_odd = {j: _dot(_odd_blocks(d_b[j], s), _unfold(_spread_odd_blocks(e_b[j], s))) for j in heads}
            dinv = {j: dinv[j] - _spread_odd_blocks(f_odd[j], s) for j in heads}
        side()
    recycle()
    rhs = {j: jnp.concatenate([gv_scr[:, hs(j)] * bc(j, 1), gk_scr[:, hs(j)] * (bc(j, 1) * bc(j, 2))], axis=1)
           for j in heads}
    sol_top = {j: _dot(dinv[j][:, :HALF].astype(BF16), rhs[j][top].astype(BF16)) for j in heads}
    side()
    res_bot = {j: (rhs[j][bot] - _dot(low[j][bot, top], sol_top[j].astype(BF16))).astype(BF16) for j in heads}
    sol_bot = {j: _dot(dinv[j][:, HALF:].astype(BF16), res_bot[j]) for j in heads}
    side()
    sol = {j: jnp.concatenate([sol_top[j], sol_bot[j]], axis=0) for j in heads}
    s_old = {p: s_scr[p] for p in pairs}
    s_bd = {p: _unfold(s_old[p].astype(BF16)) for p in pairs}
    w_s2 = {p: _dot(jnp.concatenate([sol[2 * p][:, G_DIM:], sol[2 * p + 1][:, G_DIM:]], axis=1).astype(BF16),
                    s_bd[p]) for p in pairs}
    v_new = {j: (sol[j][:, :G_DIM] - w_s2[j // 2][:, half_of(j)]).astype(BF16) for j in heads}
    side()
    for p in pairs:
        upd = []
        for j in (2 * p, 2 * p + 1):
            kd = (gk_scr[:, hs(j)] * bc(j, 3)).astype(BF16)
            upd.append(e_last[0:1, M_HEADS + j:M_HEADS + j + 1] * s_old[p][:, half_of(j)] + _dot_tn(kd, v_new[j]))
        s_scr[p] = jnp.concatenate(upd, axis=1)
    side()
    q_s2 = {p: _dot(jnp.concatenate([gq_scr[:, hs(2 * p)] * (bc(2 * p, 2) * g_scale),
                                     gq_scr[:, hs(2 * p + 1)] * (bc(2 * p + 1, 2) * g_scale)],
                                    axis=1).astype(BF16), s_bd[p]) for p in pairs}
    og = {j: q_s2[j // 2][:, half_of(j)] + _dot(a_qk[j], v_new[j]) for j in heads}
    side()
    ms_g = {j: jnp.sum(og[j] * og[j], axis=-1, keepdims=True) * (1.0 / G_DIM) for j in heads}
    for j in heads:
        z = act_ref[:, ACT_GZ + j * G_DIM:ACT_GZ + (j + 1) * G_DIM]
        y = og[j] * lax.rsqrt(ms_g[j] + EPS) * gnorm_ref[0:1, hs(j)] * _silu(z)
        mix_scr[:, M_WIDTH + j * G_DIM:M_WIDTH + (j + 1) * G_DIM] = y.astype(BF16)
        if j % 2 == 1:
            side()
    side(len(side_work))


def _layer_kernel(x_next_ref, x_cur_ref, again_ref, wa_ref, wb_ref, wg_ref, pv_ref, conv_ref, mnorm_ref, gnorm_ref,
                  mask_ref, wo_ref, fn_ref, out_ref,
                  qkv_cur, act_cur, gt_cur, qkv_next, act_next, gt_next, h_scr, mix_scr, y_scr,
                  c_scr, n_scr, m_scr, s_scr, xbuf_scr,
                  gq_cur, gk_cur, gv_cur, gq_next, gk_next, gv_next, bc_scr, *, steps):
    g = pl.program_id(0)

    @pl.when(g == 0)
    def _():
        qkv_cur[...] = jnp.zeros_like(qkv_cur)
        act_cur[...] = jnp.zeros_like(act_cur)
        gt_cur[...] = jnp.zeros_like(gt_cur)
        mix_scr[...] = jnp.zeros_like(mix_scr)
        gq_cur[...] = jnp.zeros_like(gq_cur)
        gk_cur[...] = jnp.zeros_like(gk_cur)
        gv_cur[...] = jnp.zeros_like(gv_cur)

    @pl.when(jnp.logical_or(g == 0, lax.rem(g + steps - 1, steps) == 0))
    def _():
        c_scr[...] = jnp.zeros_like(c_scr)
        n_scr[...] = jnp.zeros_like(n_scr)
        m_scr[...] = jnp.zeros_like(m_scr)
        s_scr[...] = jnp.zeros_like(s_scr)

    @pl.when(lax.rem(g, steps) == 0)
    def _():
        xbuf_scr[0:8, :] = jnp.zeros((8, 3 * G_WIDTH), F32)

    finish = _output_pieces(mix_scr, x_cur_ref, wo_ref, fn_ref, y_scr, out_ref)
    gate_piece, proj, projected = _projection_pieces(x_next_ref, again_ref, wa_ref, wb_ref, wg_ref, h_scr,
                                                     qkv_next, act_next, xbuf_scr, gt_next)
    conv = _front_end_pieces(conv_ref, xbuf_scr, (gq_next, gk_next, gv_next))
    qkv_blocks = [OFF_GQKV // PROJ_COLS + i for i in range(3 * G_WIDTH // PROJ_COLS)]
    rest = [proj[nb] for nb in sorted(proj) if nb not in qkv_blocks]
    skip = lambda: None
    side_work = (finish + [gate_piece] + [proj[nb] for nb in qkv_blocks] + [skip]
                 + [rest[0], skip, rest[1], skip, skip, rest[2], skip, skip, skip]
                 + [rest[3], skip, rest[4], skip, skip]
                 + conv[:10] + [skip, skip]
                 + conv[10:])
    assert len(rest) == 5

    def recycle():
        assert all(nb in projected for nb in range(N_QKV // PROJ_COLS)), "projection piece not issued yet"
        qkv_cur[...] = qkv_next[...]

    _mix(qkv_cur, act_cur, gt_cur, pv_ref, mnorm_ref, gnorm_ref, mask_ref, mix_scr,
         c_scr, n_scr, m_scr, s_scr, gq_cur, gk_cur, gv_cur, bc_scr, side_work, recycle)

    assert len(projected) == N_MAIN // PROJ_COLS
    act_cur[...] = act_next[...]
    gt_cur[...] = gt_next[...]
    gq_cur[...] = gq_next[...]
    gk_cur[...] = gk_next[...]
    gv_cur[...] = gv_next[...]


def _layer(x2, again, w_a, w_b, w_gate, pvec, conv_w, mnorm, gnorm, w_out, fgain, steps):
    n_tiles = x2.shape[0] // TILE
    const = lambda g: (0, 0)
    single = dict(pipeline_mode=pl.Buffered(1))
    return pl.pallas_call(
        functools.partial(_layer_kernel, steps=steps),
        grid=(n_tiles + 2,),
        in_specs=[
            pl.BlockSpec((TILE, D_MODEL), lambda g: (jnp.minimum(g, n_tiles - 1), 0)),
            pl.BlockSpec((TILE, D_MODEL), lambda g: (jnp.maximum(g - 2, 0), 0)),
            pl.BlockSpec((1, D_MODEL), const),
            pl.BlockSpec((D_MODEL, OFF_GQKV), const, **single),
            pl.BlockSpec((D_MODEL, N_MAIN - OFF_GQKV), const, **single),
            pl.BlockSpec((D_MODEL, N_GATE), const, **single),
            pl.BlockSpec((8, LANES), const),
            pl.BlockSpec((CONV_WIDTH, 3 * G_WIDTH), const),
            pl.BlockSpec((1, M_WIDTH), const),
            pl.BlockSpec((1, G_WIDTH), const),
            pl.BlockSpec((N_LEVELS + 1, HALF, TILE), lambda g: (0, 0, 0), **single),
            pl.BlockSpec((MIX_WIDTH, D_MODEL), const, **single),
            pl.BlockSpec((1, D_MODEL), const),
        ],
        out_specs=pl.BlockSpec((TILE, D_MODEL), lambda g: (jnp.maximum(g - 2, 0), 0)),
        out_shape=jax.ShapeDtypeStruct(x2.shape, F32),
        scratch_shapes=[
            pltpu.VMEM((TILE, N_QKV), BF16),
            pltpu.VMEM((TILE, N_ACT), F32),
            pltpu.VMEM((TILE, N_GATE), F32),
            pltpu.VMEM((TILE, N_QKV), BF16),
            pltpu.VMEM((TILE, N_ACT), F32),
            pltpu.VMEM((TILE, N_GATE), F32),
            pltpu.VMEM((TILE, D_MODEL), BF16),
            pltpu.VMEM((TILE, MIX_WIDTH), BF16),
            pltpu.VMEM((TILE, D_MODEL), F32),
            pltpu.VMEM((M_HEADS, M_QK_DIM, M_V_DIM), F32),
            pltpu.VMEM((M_HEADS, 8, LANES), F32),
            pltpu.VMEM((1, LANES), F32),
            pltpu.VMEM((G_HEADS // 2, G_DIM, 2 * G_DIM), F32),
            pltpu.VMEM((TILE + 8, 3 * G_WIDTH), F32),
            pltpu.VMEM((TILE, G_WIDTH), F32),
            pltpu.VMEM((TILE, G_WIDTH), F32),
            pltpu.VMEM((TILE, G_WIDTH), F32),
            pltpu.VMEM((TILE, G_WIDTH), F32),
            pltpu.VMEM((TILE, G_WIDTH), F32),
            pltpu.VMEM((TILE, G_WIDTH), F32),
            pltpu.VMEM((M_HEADS * M_BC + G_HEADS * G_BC, TILE, LANES), F32),
        ],
        compiler_params=pltpu.CompilerParams(
            dimension_semantics=("arbitrary",), vmem_limit_bytes=VMEM_LIMIT),
    )(x2, x2, again, w_a, w_b, w_gate, pvec, conv_w, mnorm, gnorm, _level_masks(), w_out, fgain)


def _pad_lanes(v):
    return jnp.pad(v.astype(F32), (0, LANES - v.shape[0]))


def kernel(x, attn_norm, w_in, m_i_bias, m_f_bias, m_out_norm, g_conv, g_a_log, g_dt_bias,
           g_out_norm, w_out, final_norm):
    batch, seq, _ = x.shape
    assert attn_norm.shape[0] == 1, "single-layer block"
    assert seq % TILE == 0
    w = w_in[0]
    splits =(M_QK_WIDTH, M_QK_WIDTH, M_WIDTH, M_WIDTH, M_WIDTH, M_HEADS, M_HEADS,
              3 * G_WIDTH, G_WIDTH, G_HEADS, G_HEADS)
    offs = [0]
    for s in splits:
        offs.append(offs[-1] + s)
    mi, mf, gb, ga = (w[:, offs[i]:offs[i + 1]] for i in (5, 6, 9, 10))
    w_a = w[:, offs[0]:offs[5]].astype(BF16)
    w_b = w[:, offs[7]:offs[9]].astype(BF16)
    assert w_a.shape[1] == OFF_GQKV and w_b.shape[1] == N_MAIN - OFF_GQKV
    zpad = jnp.zeros((D_MODEL, LANES - M_HEADS - G_HEADS), w.dtype)
    w_gate = jnp.concatenate([mi, gb, zpad, mf, ga, zpad], axis=1).astype(BF16)
    zero4 = jnp.zeros((M_HEADS,), F32)
    pvec = jnp.stack([
        _pad_lanes(m_i_bias[0]),
        _pad_lanes(jnp.concatenate([m_f_bias[0].astype(F32), g_dt_bias[0].astype(F32)])),
        _pad_lanes(jnp.concatenate([zero4, g_a_log[0].astype(F32)])),
    ] + [jnp.zeros((LANES,), F32)] * 5)
    gnorm = jnp.tile(g_out_norm[0].astype(F32), G_HEADS)[None, :]

    x2 = x.reshape(batch * seq, D_MODEL)
    out = _layer(x2, attn_norm.astype(F32), w_a, w_b, w_gate, pvec, g_conv[0].astype(F32),
                 m_out_norm.astype(F32), gnorm, w_out[0].astype(BF16),
                 final_norm.astype(F32)[None, :], seq // TILE)
    return out.reshape(batch, seq, D_MODEL)
```

```python
import functools

import jax
import jax.numpy as jnp
import numpy as np
from jax import lax
from jax.experimental import pallas as pl
from jax.experimental.pallas import tpu as pltpu

F32 = jnp.float32
BF16 = jnp.bfloat16

D_MODEL = 1024
M_HEADS = 4
M_QK_DIM = 128
M_V_DIM = 256
M_QK_WIDTH = M_HEADS * M_QK_DIM
M_WIDTH = M_HEADS * M_V_DIM
G_HEADS = 8
G_DIM = 128
G_WIDTH = G_HEADS * G_DIM
CONV_WIDTH = 4
MIX_WIDTH = M_WIDTH + G_WIDTH
EPS = 1e-6
LANES = 128

OFF_MQ = 0
OFF_MK = OFF_MQ + M_QK_WIDTH
OFF_MV = OFF_MK + M_QK_WIDTH
OFF_MO = OFF_MV + M_WIDTH
OFF_MZ = OFF_MO + M_WIDTH
OFF_GQKV = OFF_MZ + M_WIDTH
OFF_GZ = OFF_GQKV + 3 * G_WIDTH
N_MAIN = OFF_GZ + G_WIDTH
N_GATE = 2 * LANES
N_QKV = OFF_MO
ACT_MO, ACT_MZ, ACT_GZ = 0, M_WIDTH, 2 * M_WIDTH
N_ACT = 2 * M_WIDTH + G_WIDTH

PROJ_COLS = 1024
OUT_COLS = 512
CONV_COLS = 2 * LANES
TILE = 256
HALF = TILE // 2
N_LEVELS = 7
ROW_GROUP = 16
VMEM_LIMIT = 60 * 1024 * 1024

M_BC = 3
G_BC = 4


def _dot(a, b):
    return jnp.dot(a, b, preferred_element_type=F32)


def _dot_nt(a, b):
    return lax.dot_general(a, b, (((1,), (1,)), ((), ())), preferred_element_type=F32)


def _dot_tn(a, b):
    return lax.dot_general(a, b, (((0,), (0,)), ((), ())), preferred_element_type=F32)


def _scan_rows(x, row, combine, identity):
    shift = 1
    while shift < TILE:
        x = combine(x, jnp.where(row >= shift, pltpu.roll(x, shift, 0), identity))
        shift *= 2
    return x


def _softplus(x):
    return jnp.maximum(x, 0.0) + jnp.log1p(jnp.exp(-jnp.abs(x)))


def _sigmoid(x):
    return 0.5 + 0.5 * jnp.tanh(0.5 * x)


def _silu(x):
    half = 0.5 * x
    return half + half * jnp.tanh(half)


def _wide(x128):
    return jnp.concatenate([x128, x128], axis=1)


def _level_masks():
    i = np.arange(HALF)[:, None]
    j = np.arange(HALF)[None, :]
    out = []
    for l in range(N_LEVELS):
        s = 1 << l
        out.append((i // (2 * s) == j // (2 * s)) & ((i // s) % 2 == 1) & ((j // s) % 2 == 0))
    out.append(i == j)
    half = np.stack(out).astype(np.float32)
    return jnp.asarray(np.concatenate([half, half], axis=2), dtype=BF16)


def _fold(full):
    return jnp.concatenate([full[:HALF, :HALF], full[HALF:, HALF:]], axis=1)


def _unfold(folded):
    zero = jnp.zeros((HALF, HALF), folded.dtype)
    return jnp.concatenate([jnp.concatenate([folded[:, :HALF], zero], axis=1),
                            jnp.concatenate([zero, folded[:, HALF:]], axis=1)], axis=0)


def _odd_blocks(x, s):
    return jnp.concatenate([x[r:r + s] for r in range(s, HALF, 2 * s)], axis=0)


def _spread_odd_blocks(x, s):
    zero = jnp.zeros((s, x.shape[1]), x.dtype)
    parts = []
    for i in range(HALF // (2 * s)):
        parts += [zero, x[i * s:(i + 1) * s]]
    return jnp.concatenate(parts, axis=0)


def _projection_pieces(x_ref, gain_ref, wa_ref, wb_ref, wg_ref, h_scr, qkv_dst, act_dst, xbuf_scr, gt_dst):
    x = x_ref[...]
    ms = jnp.mean(x * x, axis=-1, keepdims=True)
    h_scr[...] = (x * lax.rsqrt(ms + EPS) * gain_ref[...]).astype(BF16)

    def gate_piece():
        gt_dst[...] = _dot(h_scr[...], wg_ref[...])

    issued = set()

    def main_piece(nb):
        cols = slice(nb * PROJ_COLS, (nb + 1) * PROJ_COLS)
        w_ref, first = (wa_ref, 0) if nb < OFF_GQKV // PROJ_COLS else (wb_ref, OFF_GQKV)
        w_cols = slice(nb * PROJ_COLS - first, (nb + 1) * PROJ_COLS - first)
        res = _dot(h_scr[...], w_ref[:, w_cols])
        lo = nb * PROJ_COLS
        if lo < N_QKV:
            qkv_dst[:, cols] = res.astype(BF16)
        elif OFF_GQKV <= lo < OFF_GZ:
            xbuf_scr[8:8 + TILE, lo - OFF_GQKV:lo - OFF_GQKV + PROJ_COLS] = res
        else:
            dst = {OFF_MO: ACT_MO, OFF_MZ: ACT_MZ, OFF_GZ: ACT_GZ}[lo]
            act_dst[:, dst:dst + PROJ_COLS] = res
        issued.add(nb)

    return gate_piece, {nb: functools.partial(main_piece, nb) for nb in range(N_MAIN // PROJ_COLS)}, issued


def _output_pieces(mix_scr, x_ref, wo_ref, fn_ref, y_scr, out_ref):
    def block_piece(nb):
        cols = slice(nb * OUT_COLS, (nb + 1) * OUT_COLS)
        y_scr[:, cols] = x_ref[:, cols] + _dot(mix_scr[...], wo_ref[:, cols])

    def norm_piece():
        y = y_scr[...]
        ms = jnp.mean(y * y, axis=-1, keepdims=True)
        out_ref[...] = y * lax.rsqrt(ms + EPS) * fn_ref[...]

    return [functools.partial(block_piece, nb) for nb in range(D_MODEL // OUT_COLS)] + [norm_piece]


def _front_end_pieces(conv_ref, xbuf_scr, dsts):
    def block_piece(part, cb):
        ones_bd = jnp.where(
            lax.broadcasted_iota(jnp.int32, (CONV_COLS, CONV_COLS), 0) // LANES
            == lax.broadcasted_iota(jnp.int32, (CONV_COLS, CONV_COLS), 1) // LANES, 1.0, 0.0).astype(BF16)
        cols = slice(part * G_WIDTH + cb * CONV_COLS, part * G_WIDTH + (cb + 1) * CONV_COLS)
        acc = conv_ref[CONV_WIDTH - 1:CONV_WIDTH, cols] * xbuf_scr[8:8 + TILE, cols]
        for j in range(CONV_WIDTH - 1):
            off = 8 - (CONV_WIDTH - 1) + j
            acc = acc + conv_ref[j:j + 1, cols] * xbuf_scr[off:off + TILE, cols]
        act = _silu(acc)
        if part < 2:
            act = act * lax.rsqrt(_dot((act * act).astype(BF16), ones_bd) + EPS)
        dsts[part][:, cb * CONV_COLS:(cb + 1) * CONV_COLS] = act

    def tail_piece():
        xbuf_scr[0:8, :] = xbuf_scr[TILE:TILE + 8, :]

    blocks = [functools.partial(block_piece, part, cb)
              for part in range(3) for cb in range(G_WIDTH // CONV_COLS)]
    return blocks + [tail_piece]


def _mix(qkv_ref, act_ref, gt_ref, pv_ref, mnorm_ref, gnorm_ref, mask_ref, mix_scr,
         c_scr, n_scr, m_scr, s_scr, gq_scr, gk_scr, gv_scr, bc_scr, side_work, recycle):
    def side(n=1):
        for _ in range(n):
            if side_work:
                side_work.pop(0)()

    side()
    lane = lax.broadcasted_iota(jnp.int32, (TILE, LANES), 1)
    row = lax.broadcasted_iota(jnp.int32, (TILE, LANES), 0)
    is_m = lane < M_HEADS
    a = gt_ref[:, 0:LANES] + pv_ref[0:1, :]
    b = gt_ref[:, LANES:2 * LANES] + pv_ref[1:2, :]
    log_f = -_softplus(-b)
    log_alpha = -jnp.exp(pv_ref[2:3, :]) * _softplus(b)
    ab = jnp.where(is_m, a, _sigmoid(a))
    cum = _scan_rows(jnp.where(is_m, log_f, log_alpha), row, jnp.add, 0.0)
    side()
    m_prev = m_scr[...]
    cmax = _scan_rows(ab - cum, row, jnp.maximum, -jnp.inf)
    inter = cum + m_prev
    m_t = jnp.maximum(inter, cum + cmax)
    last = cum[TILE - 1:TILE, :]
    a_s = last - cum + ab
    m_new = jnp.maximum(last + m_prev, jnp.max(a_s, axis=0, keepdims=True))
    decay_m = jnp.exp(last + m_prev - m_new)
    e_last = jnp.exp(last)
    m_scr[...] = jnp.where(is_m[0:1, :], m_new, 0.0)
    row_form = jnp.where(is_m, ab - cum + np.float32(np.log(M_QK_DIM ** -0.5)), -cum).T

    w_inter = jnp.exp(inter - m_t) * (M_QK_DIM ** -0.5)
    e_neg_mt = jnp.exp(-m_t)
    m_cols = (cum - m_t, w_inter, jnp.exp(a_s - m_new))
    g_cols = (cum, ab, jnp.exp(cum), jnp.exp(last - cum))
    for h in range(M_HEADS):
        for i, colv in enumerate(m_cols):
            bc_scr[h * M_BC + i] = jnp.broadcast_to(colv[:, h:h + 1], (TILE, LANES))
        if h % 2 == 1:
            side()
    for j in range(G_HEADS):
        for i, colv in enumerate(g_cols):
            bc_scr[M_HEADS * M_BC + j * G_BC + i] = jnp.broadcast_to(
                colv[:, M_HEADS + j:M_HEADS + j + 1], (TILE, LANES))
        if j % 2 == 1:
            side()

    ri = lax.broadcasted_iota(jnp.int32, (TILE, TILE), 0)
    ci = lax.broadcasted_iota(jnp.int32, (TILE, TILE), 1)
    incl = ri >= ci

    mh = range(M_HEADS)
    mbc = lambda h, i: bc_scr[h * M_BC + i]
    q_m = {h: qkv_ref[:, OFF_MQ + h * M_QK_DIM:OFF_MQ + (h + 1) * M_QK_DIM] for h in mh}
    k_m = {h: qkv_ref[:, OFF_MK + h * M_QK_DIM:OFF_MK + (h + 1) * M_QK_DIM] for h in mh}
    v_m = {h: qkv_ref[:, OFF_MV + h * M_V_DIM:OFF_MV + (h + 1) * M_V_DIM] for h in mh}
    p_f = {}
    for h in mh:
        p_f[h] = (jnp.where(incl, jnp.exp(_wide(mbc(h, 0)) + row_form[h:h + 1, :]), 0.0)
                  * _dot_nt(q_m[h], k_m[h]))
        if h % 2 == 1:
            side()
    p_sum = {h: jnp.sum(p_f[h], axis=-1, keepdims=True) for h in mh}
    c_old = {h: c_scr[h] for h in mh}
    n_old = {h: n_scr[h, 0:1, :] for h in mh}
    q_n = {h: jnp.sum(q_m[h].astype(F32) * n_old[h], axis=-1, keepdims=True) for h in mh}
    side()
    num = {}
    for h in mh:
        num[h] = _wide(mbc(h, 1)) * _dot(q_m[h], c_old[h].astype(BF16)) + _dot(p_f[h].astype(BF16), v_m[h])
        if h % 2 == 1:
            side()
    for h in mh:
        kw = k_m[h].astype(F32) * mbc(h, 2)
        c_scr[h] = decay_m[0:1, h:h + 1] * c_old[h] + _dot_tn(kw.astype(BF16), v_m[h])
        n_scr[h, 0:1, :] = decay_m[0:1, h:h + 1] * n_old[h] + jnp.sum(kw, axis=0, keepdims=True)
    side()
    den = {h: jnp.maximum(jnp.abs(w_inter[:, h:h + 1] * q_n[h] + p_sum[h]), e_neg_mt[:, h:h + 1]) for h in mh}
    hm = {h: num[h] * (1.0 / den[h]) for h in mh}
    ms_m = {h: jnp.sum(hm[h] * hm[h], axis=-1, keepdims=True) * (1.0 / M_V_DIM) for h in mh}
    side()
    for h in mh:
        cols = slice(h * M_V_DIM, (h + 1) * M_V_DIM)
        o_gate = _sigmoid(act_ref[:, ACT_MO + h * M_V_DIM:ACT_MO + (h + 1) * M_V_DIM])
        z = act_ref[:, ACT_MZ + h * M_V_DIM:ACT_MZ + (h + 1) * M_V_DIM]
        y = hm[h] * lax.rsqrt(ms_m[h] + EPS) * mnorm_ref[0:1, cols] * o_gate * _silu(z)
        mix_scr[:, cols] = y.astype(BF16)
        if h % 2 == 1:
            side()

    g_scale = G_DIM ** -0.5
    heads = range(G_HEADS)
    pairs = range(G_HEADS // 2)
    hs = lambda j: slice(j * G_DIM, (j + 1) * G_DIM)
    half_of = lambda j: slice((j % 2) * G_DIM, (j % 2 + 1) * G_DIM)
    bc = lambda j, i: bc_scr[M_HEADS * M_BC + j * G_BC + i]
    top, bot = slice(0, HALF), slice(HALF, TILE)
    k_b = {j: gk_scr[:, hs(j)].astype(BF16) for j in heads}
    q_b = {j: (gq_scr[:, hs(j)] * g_scale).astype(BF16) for j in heads}
    side()
    gam, a_qk, low = {}, {}, {}
    for j in heads:
        gam[j] = jnp.where(incl, jnp.exp(_wide(bc(j, 0)) + row_form[M_HEADS + j:M_HEADS + j + 1, :]), 0.0)
        a_qk[j] = (_dot_nt(q_b[j], k_b[j]) * gam[j]).astype(BF16)
        low[j] = (_wide(bc(j, 1)) * _dot_nt(k_b[j], k_b[j]) * gam[j]).astype(BF16)
        if j % 2 == 1:
            side()
    low_f = {j: _fold(low[j]) for j in heads}
    dinv = {j: (mask_ref[N_LEVELS] - low_f[j] * mask_ref[0]).astype(F32) for j in heads}
    for lvl in range(1, N_LEVELS):
        s = 1 << lvl
        d_b = {j: dinv[j].astype(BF16) for j in heads}
        if s < ROW_GROUP:
            e_b = {j: _dot(low_f[j] * mask_ref[lvl], _unfold(d_b[j])).astype(BF16) for j in heads}
            side()
            dinv = {j: dinv[j] - _dot(d_b[j], _unfold(e_b[j])) for j in heads}
        else:
            e_b = {j: _dot(_odd_blocks(low_f[j] * mask_ref[lvl], s), _unfold(d_b[j])).astype(BF16) for j in heads}
            side()
            f_odd = {j: _dot(_odd_blocks(d_b[j], s), _unfold(_spread_odd_blocks(e_b[j], s))) for j in heads}
            dinv = {j: dinv[j] - _spread_odd_blocks(f_odd[j], s) for j in heads}
        side()
    recycle()
    rhs = {j: jnp.concatenate([gv_scr[:, hs(j)] * bc(j, 1), gk_scr[:, hs(j)] * (bc(j, 1) * bc(j, 2))], axis=1)
           for j in heads}
    sol_top = {j: _dot(dinv[j][:, :HALF].astype(BF16), rhs[j][top].astype(BF16)) for j in heads}
    side()
    res_bot = {j: (rhs[j][bot] - _dot(low[j][bot, top], sol_top[j].astype(BF16))).astype(BF16) for j in heads}
    sol_bot = {j: _dot(dinv[j][:, HALF:].astype(BF16), res_bot[j]) for j in heads}
    side()
    sol = {j: jnp.concatenate([sol_top[j], sol_bot[j]], axis=0) for j in heads}
    s_old = {p: s_scr[p] for p in pairs}
    s_bd = {p: _unfold(s_old[p].astype(BF16)) for p in pairs}
    w_s2 = {p: _dot(jnp.concatenate([sol[2 * p][:, G_DIM:], sol[2 * p + 1][:, G_DIM:]], axis=1).astype(BF16),
                    s_bd[p]) for p in pairs}
    v_new = {j: (sol[j][:, :G_DIM] - w_s2[j // 2][:, half_of(j)]).astype(BF16) for j in heads}
    side()
    for p in pairs:
        upd = []
        for j in (2 * p, 2 * p + 1):
            kd = (gk_scr[:, hs(j)] * bc(j, 3)).astype(BF16)
            upd.append(e_last[0:1, M_HEADS + j:M_HEADS + j + 1] * s_old[p][:, half_of(j)] + _dot_tn(kd, v_new[j]))
        s_scr[p] = jnp.concatenate(upd, axis=1)
    side()
    q_s2 = {p: _dot(jnp.concatenate([gq_scr[:, hs(2 * p)] * (bc(2 * p, 2) * g_scale),
                                     gq_scr[:, hs(2 * p + 1)] * (bc(2 * p + 1, 2) * g_scale)],
                                    axis=1).astype(BF16), s_bd[p]) for p in pairs}
    og = {j: q_s2[j // 2][:, half_of(j)] + _dot(a_qk[j], v_new[j]) for j in heads}
    side()
    ms_g = {j: jnp.sum(og[j] * og[j], axis=-1, keepdims=True) * (1.0 / G_DIM) for j in heads}
    for j in heads:
        z = act_ref[:, ACT_GZ + j * G_DIM:ACT_GZ + (j + 1) * G_DIM]
        y = og[j] * lax.rsqrt(ms_g[j] + EPS) * gnorm_ref[0:1, hs(j)] * _silu(z)
        mix_scr[:, M_WIDTH + j * G_DIM:M_WIDTH + (j + 1) * G_DIM] = y.astype(BF16)
        if j % 2 == 1:
            side()
    side(len(side_work))


def _layer_kernel(x_next_ref, x_cur_ref, again_ref, wa_ref, wb_ref, wg_ref, pv_ref, conv_ref, mnorm_ref, gnorm_ref,
                  mask_ref, wo_ref, fn_ref, out_ref,
                  qkv_cur, act_cur, gt_cur, qkv_next, act_next, gt_next, h_scr, mix_scr, y_scr,
                  c_scr, n_scr, m_scr, s_scr, xbuf_scr,
                  gq_cur, gk_cur, gv_cur, gq_next, gk_next, bc_scr, *, steps):
    g = pl.program_id(0)

    @pl.when(g == 0)
    def _():
        qkv_cur[...] = jnp.zeros_like(qkv_cur)
        act_cur[...] = jnp.zeros_like(act_cur)
        gt_cur[...] = jnp.zeros_like(gt_cur)
        mix_scr[...] = jnp.zeros_like(mix_scr)
        gq_cur[...] = jnp.zeros_like(gq_cur)
        gk_cur[...] = jnp.zeros_like(gk_cur)
        gv_cur[...] = jnp.zeros_like(gv_cur)

    @pl.when(jnp.logical_or(g == 0, lax.rem(g + steps - 1, steps) == 0))
    def _():
        c_scr[...] = jnp.zeros_like(c_scr)
        n_scr[...] = jnp.zeros_like(n_scr)
        m_scr[...] = jnp.zeros_like(m_scr)
        s_scr[...] = jnp.zeros_like(s_scr)

    @pl.when(lax.rem(g, steps) == 0)
    def _():
        xbuf_scr[0:8, :] = jnp.zeros((8, 3 * G_WIDTH), F32)

    finish = _output_pieces(mix_scr, x_cur_ref, wo_ref, fn_ref, y_scr, out_ref)
    gate_piece, proj, projected = _projection_pieces(x_next_ref, again_ref, wa_ref, wb_ref, wg_ref, h_scr,
                                                     qkv_next, act_next, xbuf_scr, gt_next)
    conv = _front_end_pieces(conv_ref, xbuf_scr, (gq_next, gk_next, gv_cur))
    n_qk = 2 * G_WIDTH // CONV_COLS
    qkv_blocks = [OFF_GQKV // PROJ_COLS + i for i in range(3 * G_WIDTH // PROJ_COLS)]
    rest = [proj[nb] for nb in sorted(proj) if nb not in qkv_blocks]
    skip = lambda: None
    side_work = (finish + [gate_piece] + [proj[nb] for nb in qkv_blocks] + [skip]
                 + [rest[0], skip, rest[1], skip, skip, rest[2], skip, skip, skip]
                 + [rest[3], skip, rest[4], skip, skip]
                 + conv[:n_qk] + [skip] * (2 * (N_LEVELS - 1) - n_qk)
                 + conv[n_qk:])
    assert len(rest) == 5

    def recycle():
        assert all(nb in projected for nb in range(N_QKV // PROJ_COLS)), "projection piece not issued yet"
        qkv_cur[...] = qkv_next[...]

    _mix(qkv_cur, act_cur, gt_cur, pv_ref, mnorm_ref, gnorm_ref, mask_ref, mix_scr,
         c_scr, n_scr, m_scr, s_scr, gq_cur, gk_cur, gv_cur, bc_scr, side_work, recycle)

    assert len(projected) == N_MAIN // PROJ_COLS
    act_cur[...] = act_next[...]
    gt_cur[...] = gt_next[...]
    gq_cur[...] = gq_next[...]
    gk_cur[...] = gk_next[...]


def _layer(x2, again, w_a, w_b, w_gate, pvec, conv_w, mnorm, gnorm, w_out, fgain, steps):
    n_tiles = x2.shape[0] // TILE
    const = lambda g: (0, 0)
    single = dict(pipeline_mode=pl.Buffered(1))
    return pl.pallas_call(
        functools.partial(_layer_kernel, steps=steps),
        grid=(n_tiles + 2,),
        in_specs=[
            pl.BlockSpec((TILE, D_MODEL), lambda g: (jnp.minimum(g, n_tiles - 1), 0)),
            pl.BlockSpec((TILE, D_MODEL), lambda g: (jnp.maximum(g - 2, 0), 0)),
            pl.BlockSpec((1, D_MODEL), const),
            pl.BlockSpec((D_MODEL, OFF_GQKV), const, **single),
            pl.BlockSpec((D_MODEL, N_MAIN - OFF_GQKV), const, **single),
            pl.BlockSpec((D_MODEL, N_GATE), const, **single),
            pl.BlockSpec((8, LANES), const),
            pl.BlockSpec((CONV_WIDTH, 3 * G_WIDTH), const),
            pl.BlockSpec((1, M_WIDTH), const),
            pl.BlockSpec((1, G_WIDTH), const),
            pl.BlockSpec((N_LEVELS + 1, HALF, TILE), lambda g: (0, 0, 0), **single),
            pl.BlockSpec((MIX_WIDTH, D_MODEL), const, **single),
            pl.BlockSpec((1, D_MODEL), const),
        ],
        out_specs=pl.BlockSpec((TILE, D_MODEL), lambda g: (jnp.maximum(g - 2, 0), 0)),
        out_shape=jax.ShapeDtypeStruct(x2.shape, F32),
        scratch_shapes=[
            pltpu.VMEM((TILE, N_QKV), BF16),
            pltpu.VMEM((TILE, N_ACT), F32),
            pltpu.VMEM((TILE, N_GATE), F32),
            pltpu.VMEM((TILE, N_QKV), BF16),
            pltpu.VMEM((TILE, N_ACT), F32),
            pltpu.VMEM((TILE, N_GATE), F32),
            pltpu.VMEM((TILE, D_MODEL), BF16),
            pltpu.VMEM((TILE, MIX_WIDTH), BF16),
            pltpu.VMEM((TILE, D_MODEL), F32),
            pltpu.VMEM((M_HEADS, M_QK_DIM, M_V_DIM), F32),
            pltpu.VMEM((M_HEADS, 8, LANES), F32),
            pltpu.VMEM((1, LANES), F32),
            pltpu.VMEM((G_HEADS // 2, G_DIM, 2 * G_DIM), F32),
            pltpu.VMEM((TILE + 8, 3 * G_WIDTH), F32),
            pltpu.VMEM((TILE, G_WIDTH), F32),
            pltpu.VMEM((TILE, G_WIDTH), F32),
            pltpu.VMEM((TILE, G_WIDTH), F32),
            pltpu.VMEM((TILE, G_WIDTH), F32),
            pltpu.VMEM((TILE, G_WIDTH), F32),
            pltpu.VMEM((M_HEADS * M_BC + G_HEADS * G_BC, TILE, LANES), F32),
        ],
        compiler_params=pltpu.CompilerParams(
            dimension_semantics=("arbitrary",), vmem_limit_bytes=VMEM_LIMIT),
    )(x2, x2, again, w_a, w_b, w_gate, pvec, conv_w, mnorm, gnorm, _level_masks(), w_out, fgain)


def _pad_lanes(v):
    return jnp.pad(v.astype(F32), (0, LANES - v.shape[0]))


def kernel(x, attn_norm, w_in, m_i_bias, m_f_bias, m_out_norm, g_conv, g_a_log, g_dt_bias,
           g_out_norm, w_out, final_norm):
    batch, seq, _ = x.shape
    assert attn_norm.shape[0] == 1, "single-layer block"
    assert seq % TILE == 0
    w = w_in[0]
    splits =(M_QK_WIDTH, M_QK_WIDTH, M_WIDTH, M_WIDTH, M_WIDTH, M_HEADS, M_HEADS,
              3 * G_WIDTH, G_WIDTH, G_HEADS, G_HEADS)
    offs = [0]
    for s in splits:
        offs.append(offs[-1] + s)
    mi, mf, gb, ga = (w[:, offs[i]:offs[i + 1]] for i in (5, 6, 9, 10))
    w_a = w[:, offs[0]:offs[5]].astype(BF16)
    w_b = w[:, offs[7]:offs[9]].astype(BF16)
    assert w_a.shape[1] == OFF_GQKV and w_b.shape[1] == N_MAIN - OFF_GQKV
    zpad = jnp.zeros((D_MODEL, LANES - M_HEADS - G_HEADS), w.dtype)
    w_gate = jnp.concatenate([mi, gb, zpad, mf, ga, zpad], axis=1).astype(BF16)
    zero4 = jnp.zeros((M_HEADS,), F32)
    pvec = jnp.stack([
        _pad_lanes(m_i_bias[0]),
        _pad_lanes(jnp.concatenate([m_f_bias[0].astype(F32), g_dt_bias[0].astype(F32)])),
        _pad_lanes(jnp.concatenate([zero4, g_a_log[0].astype(F32)])),
    ] + [jnp.zeros((LANES,), F32)] * 5)
    gnorm = jnp.tile(g_out_norm[0].astype(F32), G_HEADS)[None, :]

    x2 = x.reshape(batch * seq, D_MODEL)
    out = _layer(x2, attn_norm.astype(F32), w_a, w_b, w_gate, pvec, g_conv[0].astype(F32),
                 m_out_norm.astype(F32), gnorm, w_out[0].astype(BF16),
                 final_norm.astype(F32)[None, :], seq // TILE)
    return out.reshape(batch, seq, D_MODEL)
```
